```python
import math
import jax, jax.numpy as jnp
from jax import lax
import numpy as np

D_MODEL = 1024
BATCH = 8
SEQ = 8192
DEPTH = 4

POOL_WINDOWS = (2, 4, 8, 16)
N_POOL_GROUPS = len(POOL_WINDOWS)
POOL_GROUP_DIM = D_MODEL // 8
POOL_W = N_POOL_GROUPS * POOL_GROUP_DIM
FOX_HEADS = 8
FOX_HEAD_DIM = 64
FOX_W = FOX_HEADS * FOX_HEAD_DIM
Q_BLOCK = 128
OFF_POOL = 0
OFF_Q = OFF_POOL + POOL_W
OFF_K = OFF_Q + FOX_W
OFF_V = OFF_K + FOX_W
OFF_F = OFF_V + FOX_W
OFF_GP = OFF_F + FOX_HEADS
OFF_GF = OFF_GP + D_MODEL
IN_W = OFF_GF + D_MODEL
N_MEM = 256
X_HEADS = 4
X_HEAD_DIM = 128
X_W = X_HEADS * X_HEAD_DIM
D_FF = 2816
CONV_WIDTH = 3
RMS_EPS = 1e-6

kernel_name = "hybrid_pool_fox_memxattn_convffn"


def rms_norm(x, g):
    xf = x.astype(jnp.float32)
    y = xf * lax.rsqrt(jnp.mean(xf * xf, axis=-1, keepdims=True) + RMS_EPS)
    return (y * g.astype(jnp.float32)).astype(x.dtype)


def shift_right(z, n):
    return jnp.pad(z, ((0, 0), (n, 0), (0, 0)))[:, : z.shape[1]]


def pool_mixer(u, pool_w, pool_scale):
    B, S, _ = u.shape
    uf = u.astype(jnp.float32)
    cs = jnp.cumsum(uf, axis=1)
    t = jnp.arange(S)
    outs = []
    for g, w in enumerate(POOL_WINDOWS):
        sl = slice(g * POOL_GROUP_DIM, (g + 1) * POOL_GROUP_DIM)
        csg = cs[..., sl]
        cnt = jnp.minimum(t + 1, w).astype(jnp.float32)[None, :, None]
        outs.append((csg - shift_right(csg, w)) / cnt - uf[..., sl])
    pooled = jnp.stack(outs, axis=2).astype(u.dtype)
    mixed = jnp.einsum("bsgc,gcd->bsgd", pooled, pool_w).reshape(B, S, POOL_W)
    return mixed * pool_scale


def fox_attention(q, k, v, log_f):
    B, S, H, Dh = q.shape
    nb = S // Q_BLOCK
    scale = 1.0 / math.sqrt(Dh)
    cT = jnp.cumsum(log_f.astype(jnp.float32), axis=1).transpose(0, 2, 1)
    qb = q.reshape(B, nb, Q_BLOCK, H, Dh).transpose(1, 0, 2, 3, 4)
    cb = cT.reshape(B, H, nb, Q_BLOCK).transpose(2, 0, 1, 3)
    kpos = jnp.arange(S)

    def one_block(args):
        qi, ci, bi = args
        s = jnp.einsum("bqhd,bkhd->bhqk", qi, k, preferred_element_type=jnp.float32) * scale
        s = s + ci[..., :, None] - cT[:, :, None, :]
        qpos = bi * Q_BLOCK + jnp.arange(Q_BLOCK)
        s = jnp.where(kpos[None, :] <= qpos[:, None], s, -jnp.inf)
        p = jax.nn.softmax(s, axis=-1)
        return jnp.einsum("bhqk,bkhd->bqhd", p.astype(v.dtype), v)

    out = lax.map(one_block, (qb, cb, jnp.arange(nb)))
    return out.transpose(1, 0, 2, 3, 4).reshape(B, S, H * Dh)


def mem_attention(h, mem_n, w_xq, w_xkv, w_xo):
    B, S, _ = h.shape
    M = mem_n.shape[1]
    q = (h @ w_xq).reshape(B, S, X_HEADS, X_HEAD_DIM)
    kv = mem_n @ w_xkv
    k = kv[..., :X_W].reshape(B, M, X_HEADS, X_HEAD_DIM)
    v = kv[..., X_W:].reshape(B, M, X_HEADS, X_HEAD_DIM)
    s = jnp.einsum("bqhd,bmhd->bhqm", q, k, preferred_element_type=jnp.float32) / math.sqrt(X_HEAD_DIM)
    p = jax.nn.softmax(s, axis=-1)
    o = jnp.einsum("bhqm,bmhd->bqhd", p.astype(v.dtype), v).reshape(B, S, X_W)
    return o @ w_xo


def conv_ffn(h, w_up, conv_w, conv_b, w_down):
    z = h @ w_up
    zc = conv_w[2] * z + conv_w[1] * shift_right(z, 1) + conv_w[0] * shift_right(z, 2) + conv_b
    g, u = zc[..., :D_FF], zc[..., D_FF:]
    return (jax.nn.gelu(g, approximate=True) * u) @ w_down


def setup_inputs(seed: int = 0) -> dict:
    key = jax.random.key(seed)
    ks = jax.random.split(key, 32)
    L, D = DEPTH, D_MODEL
    f32 = jnp.float32

    def nrm(k, shape, fan_in):
        return jax.random.normal(k, shape, f32) * (fan_in ** -0.5)

    def gain(k):
        return 1.0 + 0.05 * jax.random.normal(k, (L, D), f32)

    b_forget = (jnp.linspace(1.0, 5.0, FOX_HEADS, dtype=f32)[None, :]
                + 0.1 * jax.random.normal(ks[5], (L, FOX_HEADS), f32))
    return {
        "x": jax.random.normal(ks[0], (BATCH, SEQ, D), f32),
        "mem": jax.random.normal(ks[1], (BATCH, N_MEM, D), f32),
        "mix_pre_g": gain(ks[2]),
        "mix_post_g": gain(ks[3]),
        "w_in": nrm(ks[4], (L, D, IN_W), D),
        "b_forget": b_forget,
        "pool_w": nrm(ks[6], (L, N_POOL_GROUPS, POOL_GROUP_DIM, POOL_GROUP_DIM), POOL_GROUP_DIM),
        "pool_scale": 1.0 + 0.1 * jax.random.normal(ks[7], (L, POOL_W), f32),
        "w_pool_br": nrm(ks[8], (L, POOL_W, D), POOL_W),
        "w_fox_br": nrm(ks[9], (L, FOX_W, D), FOX_W),
        "w_mix_out": nrm(ks[10], (L, D, D), D),
        "xa_pre_g": gain(ks[11]),
        "xa_post_g": gain(ks[12]),
        "mem_g": gain(ks[13]),
        "w_xq": nrm(ks[14], (L, D, X_W), D),
        "w_xkv": nrm(ks[15], (L, D, 2 * X_W), D),
        "w_xo": nrm(ks[16], (L, X_W, D), X_W),
        "ffn_pre_g": gain(ks[17]),
        "ffn_post_g": gain(ks[18]),
        "w_up": nrm(ks[19], (L, D, 2 * D_FF), D),
        "conv_w": nrm(ks[20], (L, CONV_WIDTH, 2 * D_FF), CONV_WIDTH),
        "conv_b": 0.01 * jax.random.normal(ks[21], (L, 2 * D_FF), f32),
        "w_down": nrm(ks[22], (L, D_FF, D), D_FF),
    }


def reference(x, mem, mix_pre_g, mix_post_g, w_in, b_forget, pool_w, pool_scale,
              w_pool_br, w_fox_br, w_mix_out, xa_pre_g, xa_post_g, mem_g, w_xq, w_xkv,
              w_xo, ffn_pre_g, ffn_post_g, w_up, conv_w, conv_b, w_down):
    B, S, _ = x.shape
    for l in range(DEPTH):
        h = rms_norm(x, mix_pre_g[l])
        z = h @ w_in[l]
        u_pool = z[..., OFF_POOL:OFF_Q]
        q = z[..., OFF_Q:OFF_K].reshape(B, S, FOX_HEADS, FOX_HEAD_DIM)
        k = z[..., OFF_K:OFF_V].reshape(B, S, FOX_HEADS, FOX_HEAD_DIM)
        v = z[..., OFF_V:OFF_F].reshape(B, S, FOX_HEADS, FOX_HEAD_DIM)
        log_f = jax.nn.log_sigmoid((z[..., OFF_F:OFF_GP] + b_forget[l]).astype(jnp.float32))
        gate_pool = jax.nn.sigmoid(z[..., OFF_GP:OFF_GF])
        gate_fox = jax.nn.sigmoid(z[..., OFF_GF:])
        y_pool = pool_mixer(u_pool, pool_w[l], pool_scale[l]) @ w_pool_br[l]
        y_fox = fox_attention(q, k, v, log_f) @ w_fox_br[l]
        merged = gate_pool * y_pool + gate_fox * y_fox
        x = x + rms_norm(merged @ w_mix_out[l], mix_post_g[l])
        h = rms_norm(x, xa_pre_g[l])
        mem_n = rms_norm(mem, mem_g[l])
        x = x + rms_norm(mem_attention(h, mem_n, w_xq[l], w_xkv[l], w_xo[l]), xa_post_g[l])
        h = rms_norm(x, ffn_pre_g[l])
        x = x + rms_norm(conv_ffn(h, w_up[l], conv_w[l], conv_b[l], w_down[l]), ffn_post_g[l])
    return x
```

```python
import functools
import math

import jax
import jax.numpy as jnp
import numpy as np
from jax import lax
from jax.experimental import pallas as pl
from jax.experimental.pallas import tpu as pltpu

F32 = jnp.float32
BF16 = jnp.bfloat16

D_MODEL = 1024
POOL_WINDOWS = (2, 4, 8, 16)
POOL_GROUP_DIM = 128
POOL_W = len(POOL_WINDOWS) * POOL_GROUP_DIM
MAX_POOL_WINDOW = max(POOL_WINDOWS)
FOX_HEADS = 8
FOX_HEAD_DIM = 64
FOX_W = FOX_HEADS * FOX_HEAD_DIM
OFF_Q = POOL_W
OFF_K = OFF_Q + FOX_W
OFF_V = OFF_K + FOX_W
OFF_F = OFF_V + FOX_W
OFF_GP = OFF_F + FOX_HEADS
OFF_GF = OFF_GP + D_MODEL
N_MEM = 256
X_HEADS = 4
X_HEAD_DIM = 128
X_W = X_HEADS * X_HEAD_DIM
D_FF = 2816
RMS_EPS = 1e-6

V7X_LANES = 128
V7X_SUBLANES = 8
V7X_VMEM_BYTES = 64 * 1024 * 1024

N_PIECES = 3
CUM_GROUPS = N_PIECES * N_PIECES
CUM_LANES = V7X_LANES
NEG_BIG = -1e30
FFN_CHUNK = 256
N_FFN_CHUNKS = D_FF // FFN_CHUNK


def _seq_tile(seq_len):
    return min(512, seq_len)


def _vmem_limit(nbytes):
    return int(min(V7X_VMEM_BYTES - 4 * 1024 * 1024, nbytes))


def _rms(xf, g):
    return xf * lax.rsqrt(jnp.mean(xf * xf, axis=-1, keepdims=True) + RMS_EPS) * g


def _dot(a, b):
    return jnp.dot(a, b, preferred_element_type=F32)


def _dot_nt(a, b):
    return lax.dot_general(a, b, (((1,), (1,)), ((), ())), preferred_element_type=F32)


def _sigmoid(z):
    return 1.0 / (1.0 + jnp.exp(-z))


def _log_sigmoid(z):
    return jnp.minimum(z, 0.0) - jnp.log(1.0 + jnp.exp(-jnp.abs(z)))


def _bf16_pieces(v):
    p1 = v.astype(BF16)
    r1 = v - p1.astype(F32)
    p2 = r1.astype(BF16)
    r2 = r1 - p2.astype(F32)
    return p1, p2, r2.astype(BF16)


def _memkv_kernel(mem_ref, g_ref, w_ref, kv_ref):
    mem_n = _rms(mem_ref[0], g_ref[0])
    kv_ref[0, 0] = _dot(mem_n.astype(BF16), w_ref[0]).astype(BF16)


def _memkv(mem, mem_g, w_xkv):
    n_layers = w_xkv.shape[0]
    batch = mem.shape[0]
    return pl.pallas_call(
        _memkv_kernel,
        grid=(n_layers, batch),
        in_specs=[
            pl.BlockSpec((1, N_MEM, D_MODEL), lambda l, b: (b, 0, 0)),
            pl.BlockSpec((1, 1, D_MODEL), lambda l, b: (l, 0, 0)),
            pl.BlockSpec((1, D_MODEL, 2 * X_W), lambda l, b: (l, 0, 0)),
        ],
        out_specs=pl.BlockSpec((1, 1, N_MEM, 2 * X_W), lambda l, b: (l, b, 0, 0)),
        out_shape=jax.ShapeDtypeStruct((n_layers, batch, N_MEM, 2 * X_W), BF16),
        compiler_params=pltpu.CompilerParams(dimension_semantics=("arbitrary", "arbitrary")),
        name="mem_kv",
    )(mem, mem_g.reshape(n_layers, 1, D_MODEL), w_xkv)


def _mixin_kernel(x_ref, g_ref, wnat_ref, wf_ref, wvt_ref, bf_ref, pwbd_ref, pscale_ref, wpbr_ref,
                  q_ref, k_ref, vt_ref, kc_ref, gpy_ref, gf_ref, ubuf_ref, ccarry_ref, *, ts):
    si = pl.program_id(1)
    carry_rows = MAX_POOL_WINDOW

    @pl.when(si == 0)
    def _():
        ubuf_ref[0:carry_rows, :] = jnp.zeros((carry_rows, POOL_W), F32)
        ccarry_ref[...] = jnp.zeros_like(ccarry_ref)

    h = _rms(x_ref[0], g_ref[...]).astype(BF16)

    u = _dot(h, wnat_ref[:, 0:POOL_W])
    ubuf_ref[carry_rows:carry_rows + ts, :] = u
    pos = si * ts + lax.broadcasted_iota(jnp.int32, (ts, 1), 0)
    mixed = []
    for pair in range(len(POOL_WINDOWS) // 2):
        pooled_pair = []
        for gi in (2 * pair, 2 * pair + 1):
            w = POOL_WINDOWS[gi]
            cols = pl.ds(gi * POOL_GROUP_DIM, POOL_GROUP_DIM)
            win = ubuf_ref[pl.ds(carry_rows, ts), cols]
            for j in range(1, w):
                win = win + ubuf_ref[pl.ds(carry_rows - j, ts), cols]
            cnt = jnp.minimum(pos + 1, w).astype(F32)
            pooled = win / cnt - ubuf_ref[pl.ds(carry_rows, ts), cols]
            pooled_pair.append(pooled.astype(BF16))
        mixed.append(_dot(jnp.concatenate(pooled_pair, axis=1), pwbd_ref[pair]))
    ubuf_ref[0:carry_rows, :] = ubuf_ref[ts:ts + carry_rows, :]
    mixed = jnp.concatenate(mixed, axis=1) * pscale_ref[...]
    y_pool = _dot(mixed.astype(BF16), wpbr_ref[...])

    gate_pool = _sigmoid(_dot(h, wnat_ref[:, 3 * FOX_W:3 * FOX_W + D_MODEL]))
    gpy_ref[0] = (gate_pool * y_pool).astype(BF16)
    gate_fox = _sigmoid(_dot(h, wnat_ref[:, 3 * FOX_W + D_MODEL:3 * FOX_W + 2 * D_MODEL]))
    gf_ref[0] = gate_fox.astype(BF16)

    q_ref[0] = (_dot(h, wnat_ref[:, OFF_Q:OFF_K]) * (1.0 / math.sqrt(FOX_HEAD_DIM))).astype(BF16)
    k_ref[0] = _dot(h, wnat_ref[:, OFF_K:OFF_V]).astype(BF16)
    vt_ref[0] = _dot_nt(wvt_ref[...], h).astype(BF16)

    logf = _log_sigmoid(_dot(h, wf_ref[...]) + bf_ref[...])
    grp = lax.broadcasted_iota(jnp.int32, (1, CUM_LANES), 1) // FOX_HEADS
    live = grp < CUM_GROUPS
    p1, p2, p3 = _bf16_pieces(logf)
    src = grp // N_PIECES
    xin = jnp.where(src == 0, p1, jnp.where(src == 1, p2, p3))
    xin = jnp.where(live, xin, jnp.zeros_like(xin))
    tri = (lax.broadcasted_iota(jnp.int32, (ts, ts), 0)
           >= lax.broadcasted_iota(jnp.int32, (ts, ts), 1)).astype(BF16)
    cum = _dot(tri, xin) + ccarry_ref[0:1, :]
    ccarry_ref[0:1, :] = cum[ts - 1:ts, :]
    s1, s2, s3 = _bf16_pieces(cum)
    sel = grp % N_PIECES
    kc = jnp.where(sel == 0, s1, jnp.where(sel == 1, s2, s3))
    kc_ref[0] = jnp.where(live, kc, jnp.zeros_like(kc))


def _mix_in(x, g_pre, w_nat, w_f, w_vt, b_f, pw_bd, pool_scale, w_pool_br):
    batch, seq, _ = x.shape
    ts = _seq_tile(seq)
    n_nat = w_nat.shape[1]
    const = lambda b, s: (0, 0)
    const3 = lambda b, s: (0, 0, 0)
    one = pl.Buffered(1)
    tok = lambda width: pl.BlockSpec((1, ts, width), lambda b, s: (b, s, 0))
    out_shapes = (
        jax.ShapeDtypeStruct((batch, seq, FOX_W), BF16),
        jax.ShapeDtypeStruct((batch, seq, FOX_W), BF16),
        jax.ShapeDtypeStruct((batch, FOX_W, seq), BF16),
        jax.ShapeDtypeStruct((batch, seq, CUM_LANES), BF16),
        jax.ShapeDtypeStruct((batch, seq, D_MODEL), BF16),
        jax.ShapeDtypeStruct((batch, seq, D_MODEL), BF16),
    )
    return pl.pallas_call(
        functools.partial(_mixin_kernel, ts=ts),
        grid=(batch, seq // ts),
        in_specs=[
            tok(D_MODEL),
            pl.BlockSpec((1, D_MODEL), const),
            pl.BlockSpec((D_MODEL, n_nat), const, pipeline_mode=one),
            pl.BlockSpec((D_MODEL, CUM_LANES), const, pipeline_mode=one),
            pl.BlockSpec((FOX_W, D_MODEL), const, pipeline_mode=one),
            pl.BlockSpec((1, CUM_LANES), const),
            pl.BlockSpec((2, 2 * POOL_GROUP_DIM, 2 * POOL_GROUP_DIM), const3, pipeline_mode=one),
            pl.BlockSpec((1, POOL_W), const),
            pl.BlockSpec((POOL_W, D_MODEL), const, pipeline_mode=one),
        ],
        out_specs=(
            tok(FOX_W), tok(FOX_W),
            pl.BlockSpec((1, FOX_W, ts), lambda b, s: (b, 0, s)),
            tok(CUM_LANES), tok(D_MODEL), tok(D_MODEL),
        ),
        out_shape=out_shapes,
        scratch_shapes=[
            pltpu.VMEM((MAX_POOL_WINDOW + ts, POOL_W), F32),
            pltpu.VMEM((V7X_SUBLANES, CUM_LANES), F32),
        ],
        compiler_params=pltpu.CompilerParams(
            dimension_semantics=("arbitrary", "arbitrary"),
            vmem_limit_bytes=_vmem_limit(56 * 1024 * 1024)),
        name="mix_in",
    )(x, g_pre, w_nat, w_f, w_vt, b_f, pw_bd, pool_scale, w_pool_br)


def _fox_kernel(qi_ref, ki_ref, q_ref, k_ref, kc_ref, vt_ref, o_ref, acc_ref, m_ref, l_ref, *, t):
    step = pl.program_id(1)
    qi = qi_ref[step]
    ki = ki_ref[step]

    @pl.when(ki == 0)
    def _():
        acc_ref[...] = jnp.zeros_like(acc_ref)
        m_ref[...] = jnp.full_like(m_ref, NEG_BIG)
        l_ref[...] = jnp.zeros_like(l_ref)

    lane = lax.broadcasted_iota(jnp.int32, (1, 2 * FOX_HEAD_DIM), 1)
    cum_lane = lax.broadcasted_iota(jnp.int32, (1, CUM_LANES), 1)

    def process(masked):
        kcb = kc_ref[0]
        if masked:
            causal = (lax.broadcasted_iota(jnp.int32, (t, t), 0)
                      <= lax.broadcasted_iota(jnp.int32, (t, t), 1))
        for pair in range(FOX_HEADS // 2):
            cols = slice(pair * 2 * FOX_HEAD_DIM, (pair + 1) * 2 * FOX_HEAD_DIM)
            lhs = jnp.concatenate([k_ref[0, :, cols], kcb], axis=1)
            qp = q_ref[0, :, cols]
            for a in range(2):
                head = 2 * pair + a
                in_head = (lane // FOX_HEAD_DIM) == a
                qa = jnp.where(in_head, qp, jnp.zeros_like(qp))
                pick = jnp.where((cum_lane % FOX_HEADS == head)
                                 & (cum_lane // FOX_HEADS < CUM_GROUPS), -1.0, 0.0)
                pick = jnp.broadcast_to(pick, (t, CUM_LANES)).astype(BF16)
                rhs = jnp.concatenate([qa, pick], axis=1)
                st = _dot_nt(lhs, rhs)
                if masked:
                    st = jnp.where(causal, st, NEG_BIG)
                m_old = m_ref[head:head + 1, :]
                m_new = jnp.maximum(m_old, jnp.max(st, axis=0, keepdims=True))
                p = jnp.exp(st - m_new)
                alpha = jnp.exp(m_old - m_new)
                l_ref[head:head + 1, :] = alpha * l_ref[head:head + 1, :] + jnp.sum(p, axis=0, keepdims=True)
                m_ref[head:head + 1, :] = m_new
                rows = slice(head * FOX_HEAD_DIM, (head + 1) * FOX_HEAD_DIM)
                pv = _dot(vt_ref[0, rows, :], p.astype(BF16))
                acc_ref[rows, :] = alpha * acc_ref[rows, :] + pv

    @pl.when(ki < qi)
    def _():
        process(False)

    @pl.when(ki == qi)
    def _():
        process(True)
        inv_l = 1.0 / l_ref[...]
        outs = []
        for head in range(FOX_HEADS):
            rows = slice(head * FOX_HEAD_DIM, (head + 1) * FOX_HEAD_DIM)
            outs.append(acc_ref[rows, :] * inv_l[head:head + 1, :])
        o_ref[0] = jnp.concatenate(outs, axis=0).T.astype(BF16)


def _fox_attention(q, k, kc, vt):
    batch, seq, _ = q.shape
    t = _seq_tile(seq)
    nq = seq // t
    qi_tab = np.concatenate([np.full(i + 1, i, np.int32) for i in range(nq)])
    ki_tab = np.concatenate([np.arange(i + 1, dtype=np.int32) for i in range(nq)])
    grid_spec = pltpu.PrefetchScalarGridSpec(
        num_scalar_prefetch=2,
        grid=(batch, int(qi_tab.shape[0])),
        in_specs=[
            pl.BlockSpec((1, t, FOX_W), lambda b, s, qi, ki: (b, qi[s], 0)),
            pl.BlockSpec((1, t, FOX_W), lambda b, s, qi, ki: (b, ki[s], 0)),
            pl.BlockSpec((1, t, CUM_LANES), lambda b, s, qi, ki: (b, ki[s], 0)),
            pl.BlockSpec((1, FOX_W, t), lambda b, s, qi, ki: (b, 0, ki[s])),
        ],
        out_specs=pl.BlockSpec((1, t, FOX_W), lambda b, s, qi, ki: (b, qi[s], 0)),
        scratch_shapes=[
            pltpu.VMEM((FOX_W, t), F32),
            pltpu.VMEM((FOX_HEADS, t), F32),
            pltpu.VMEM((FOX_HEADS, t), F32),
        ],
    )
    return pl.pallas_call(
        functools.partial(_fox_kernel, t=t),
        grid_spec=grid_spec,
        out_shape=jax.ShapeDtypeStruct((batch, seq, FOX_W), BF16),
        compiler_params=pltpu.CompilerParams(
            dimension_semantics=("arbitrary", "arbitrary"),
            vmem_limit_bytes=_vmem_limit(56 * 1024 * 1024)),
        name="fox_attention",
    )(jnp.asarray(qi_tab), jnp.asarray(ki_tab), q, k, kc, vt)


def _mixout_kernel(o_ref, gpy_ref, gf_ref, x_ref, kv_ref, wfbr_ref, wmo_ref, wxq_ref, wxo_ref,
                   g_post_ref, g_xpre_ref, g_xpost_ref, out_ref):
    y_fox = _dot(o_ref[0], wfbr_ref[...])
    merged = gpy_ref[0].astype(F32) + gf_ref[0].astype(F32) * y_fox
    x1 = x_ref[0] + _rms(_dot(merged.astype(BF16), wmo_ref[...]), g_post_ref[...])

    h = _rms(x1, g_xpre_ref[...]).astype(BF16)
    qx = _dot(h, wxq_ref[...]).astype(BF16)
    kv = kv_ref[0, 0]
    heads = []
    for head in range(X_HEADS):
        cols = slice(head * X_HEAD_DIM, (head + 1) * X_HEAD_DIM)
        vcols = slice(X_W + head * X_HEAD_DIM, X_W + (head + 1) * X_HEAD_DIM)
        s = _dot_nt(qx[:, cols], kv[:, cols]) * (1.0 / math.sqrt(X_HEAD_DIM))
        p = jnp.exp(s - jnp.max(s, axis=-1, keepdims=True))
        inv_l = 1.0 / jnp.sum(p, axis=-1, keepdims=True)
        heads.append((_dot(p.astype(BF16), kv[:, vcols]) * inv_l).astype(BF16))
    xo = _dot(jnp.concatenate(heads, axis=1), wxo_ref[...])
    out_ref[0] = x1 + _rms(xo, g_xpost_ref[...])


def _mix_out(o, gpy, gf, x, kv_all, layer, w_fox_br, w_mix_out, w_xq, w_xo, g_post, g_xpre, g_xpost):
    batch, seq, _ = x.shape
    ts = _seq_tile(seq)
    const = lambda b, s: (0, 0)
    one = pl.Buffered(1)
    tok = lambda width: pl.BlockSpec((1, ts, width), lambda b, s: (b, s, 0))
    return pl.pallas_call(
        _mixout_kernel,
        grid=(batch, seq // ts),
        in_specs=[
            tok(FOX_W), tok(D_MODEL), tok(D_MODEL), tok(D_MODEL),
            pl.BlockSpec((1, 1, N_MEM, 2 * X_W), lambda b, s: (layer, b, 0, 0)),
            pl.BlockSpec((FOX_W, D_MODEL), const, pipeline_mode=one),
            pl.BlockSpec((D_MODEL, D_MODEL), const, pipeline_mode=one),
            pl.BlockSpec((D_MODEL, X_W), const, pipeline_mode=one),
            pl.BlockSpec((X_W, D_MODEL), const, pipeline_mode=one),
            pl.BlockSpec((1, D_MODEL), const),
            pl.BlockSpec((1, D_MODEL), const),
            pl.BlockSpec((1, D_MODEL), const),
        ],
        out_specs=tok(D_MODEL),
        out_shape=jax.ShapeDtypeStruct((batch, seq, D_MODEL), F32),
        compiler_params=pltpu.CompilerParams(
            dimension_semantics=("arbitrary", "arbitrary"),
            vmem_limit_bytes=_vmem_limit(56 * 1024 * 1024)),
        name="mix_out_xattn",
    )(o, gpy, gf, x, kv_all, w_fox_br, w_mix_out, w_xq, w_xo, g_post, g_xpre, g_xpost)


def _ffn_kernel(x_ref, g_pre_ref, g_post_ref, wg_ref, wu_ref, cwg_ref, cwu_ref, cbg_ref, cbu_ref,
                wd_ref, out_ref, carry_g_ref, carry_u_ref, stage_ref, act_ref, *, ts):
    si = pl.program_id(1)
    pad = V7X_SUBLANES

    @pl.when(si == 0)
    def _():
        carry_g_ref[...] = jnp.zeros_like(carry_g_ref)
        carry_u_ref[...] = jnp.zeros_like(carry_u_ref)

    x = x_ref[0]
    h = _rms(x, g_pre_ref[...]).astype(BF16)

    def conv(z, carry_ref, cw_ref, cb_ref, c, slot):
        stage_ref[slot, 0:pad, :] = carry_ref[c]
        stage_ref[slot, pad:pad + ts, :] = z
        carry_ref[c] = z[ts - pad:ts, :]
        z1 = stage_ref[slot, pl.ds(pad - 1, ts), :]
        z2 = stage_ref[slot, pl.ds(pad - 2, ts), :]
        cw = cw_ref[c]
        return cw[2:3, :] * z + cw[1:2, :] * z1 + cw[0:1, :] * z2 + cb_ref[c]

    for c in range(N_FFN_CHUNKS):
        zg = conv(_dot(h, wg_ref[c]), carry_g_ref, cwg_ref, cbg_ref, c, 0)
        zu = conv(_dot(h, wu_ref[c]), carry_u_ref, cwu_ref, cbu_ref, c, 1)
        act = jax.nn.gelu(zg, approximate=True) * zu
        act_ref[:, c * FFN_CHUNK:(c + 1) * FFN_CHUNK] = act.astype(BF16)

    y = _dot(act_ref[...], wd_ref[...])
    out_ref[0] = x + _rms(y, g_post_ref[...])


def _conv_ffn(x, g_pre, g_post, w_gate, w_up, cw_g, cw_u, cb_g, cb_u, w_down):
    batch, seq, _ = x.shape
    ts = _seq_tile(seq)
    const = lambda b, s: (0, 0)
    const3 = lambda b, s: (0, 0, 0)
    one = pl.Buffered(1)
    tok = pl.BlockSpec((1, ts, D_MODEL), lambda b, s: (b, s, 0))
    chunked = lambda rows, cols: pl.BlockSpec((N_FFN_CHUNKS, rows, cols), const3, pipeline_mode=one)
    return pl.pallas_call(
        functools.partial(_ffn_kernel, ts=ts),
        grid=(batch, seq // ts),
        in_specs=[
            tok,
            pl.BlockSpec((1, D_MODEL), const),
            pl.BlockSpec((1, D_MODEL), const),
            chunked(D_MODEL, FFN_CHUNK), chunked(D_MODEL, FFN_CHUNK),
            chunked(3, FFN_CHUNK), chunked(3, FFN_CHUNK),
            chunked(1, FFN_CHUNK), chunked(1, FFN_CHUNK),
            pl.BlockSpec((D_FF, D_MODEL), const, pipeline_mode=one),
        ],
        out_specs=tok,
        out_shape=jax.ShapeDtypeStruct((batch, seq, D_MODEL), F32),
        scratch_shapes=[
            pltpu.VMEM((N_FFN_CHUNKS, V7X_SUBLANES, FFN_CHUNK), F32),
            pltpu.VMEM((N_FFN_CHUNKS, V7X_SUBLANES, FFN_CHUNK), F32),
            pltpu.VMEM((2, V7X_SUBLANES + ts, FFN_CHUNK), F32),
            pltpu.VMEM((ts, D_FF), BF16),
        ],
        compiler_params=pltpu.CompilerParams(
            dimension_semantics=("arbitrary", "arbitrary"),
            vmem_limit_bytes=_vmem_limit(56 * 1024 * 1024)),
        name="conv_ffn",
    )(x, g_pre, g_post, w_gate, w_up, cw_g, cw_u, cb_g, cb_u, w_down)


def _chunk_cols(w):
    rows = w.shape[0]
    return w.reshape(rows, N_FFN_CHUNKS, FFN_CHUNK).transpose(1, 0, 2)


def _cum_lane_layout(w):
    rep = jnp.concatenate([w] * CUM_GROUPS, axis=-1)
    pad = [(0, 0)] * (w.ndim - 1) + [(0, CUM_LANES - CUM_GROUPS * FOX_HEADS)]
    return jnp.pad(rep, pad)


def _block_diag_pairs(pw):
    z = jnp.zeros((POOL_GROUP_DIM, POOL_GROUP_DIM), pw.dtype)
    return jnp.stack([
        jnp.concatenate([jnp.concatenate([pw[2 * i], z], axis=1),
                         jnp.concatenate([z, pw[2 * i + 1]], axis=1)], axis=0)
        for i in range(len(POOL_WINDOWS) // 2)])


def kernel(x, mem, mix_pre_g, mix_post_g, w_in, b_forget, pool_w, pool_scale, w_pool_br, w_fox_br,
           w_mix_out, xa_pre_g, xa_post_g, mem_g, w_xq, w_xkv, w_xo, ffn_pre_g, ffn_post_g, w_up,
           conv_w, conv_b, w_down):
    n_layers = w_in.shape[0]
    row = lambda v: v.reshape(1, -1)
    kv_all = _memkv(mem, mem_g, w_xkv.astype(BF16))
    for l in range(n_layers):
        w_in_l = w_in[l]
        w_nat = jnp.concatenate([w_in_l[:, :OFF_V], w_in_l[:, OFF_GP:]], axis=1).astype(BF16)
        w_f = _cum_lane_layout(w_in_l[:, OFF_F:OFF_GP]).astype(BF16)
        w_vt = w_in_l[:, OFF_V:OFF_F].T.astype(BF16)
        b_f = _cum_lane_layout(b_forget[l]).reshape(1, CUM_LANES)
        q, k, vt, kc, gpy, gf = _mix_in(
            x, row(mix_pre_g[l]), w_nat, w_f, w_vt, b_f, _block_diag_pairs(pool_w[l]).astype(BF16),
            row(pool_scale[l]), w_pool_br[l].astype(BF16))
        o = _fox_attention(q, k, kc, vt)
        x = _mix_out(o, gpy, gf, x, kv_all, l, w_fox_br[l].astype(BF16), w_mix_out[l].astype(BF16),
                     w_xq[l].astype(BF16), w_xo[l].astype(BF16), row(mix_post_g[l]),
                     row(xa_pre_g[l]), row(xa_post_g[l]))
        x = _conv_ffn(
            x, row(ffn_pre_g[l]), row(ffn_post_g[l]),
            _chunk_cols(w_up[l][:, :D_FF]).astype(BF16), _chunk_cols(w_up[l][:, D_FF:]).astype(BF16),
            _chunk_cols(conv_w[l][:, :D_FF]), _chunk_cols(conv_w[l][:, D_FF:]),
            _chunk_cols(conv_b[l][None, :D_FF]), _chunk_cols(conv_b[l][None, D_FF:]),
            w_down[l].astype(BF16))
    return x
```

```python
import functools
import math

import jax
import jax.numpy as jnp
import numpy as np
from jax import lax
from jax.experimental import pallas as pl
from jax.experimental.pallas import tpu as pltpu

F32 = jnp.float32
BF16 = jnp.bfloat16

D_MODEL = 1024
POOL_WINDOWS = (2, 4, 8, 16)
POOL_GROUP_DIM = 128
POOL_W = len(POOL_WINDOWS) * POOL_GROUP_DIM
MAX_POOL_WINDOW = max(POOL_WINDOWS)
FOX_HEADS = 8
FOX_HEAD_DIM = 64
FOX_W = FOX_HEADS * FOX_HEAD_DIM
OFF_Q = POOL_W
OFF_K = OFF_Q + FOX_W
OFF_V = OFF_K + FOX_W
OFF_F = OFF_V + FOX_W
OFF_GP = OFF_F + FOX_HEADS
OFF_GF = OFF_GP + D_MODEL
N_MEM = 256
X_HEADS = 4
X_HEAD_DIM = 128
X_W = X_HEADS * X_HEAD_DIM
D_FF = 2816
RMS_EPS = 1e-6

V7X_LANES = 128
V7X_SUBLANES = 8
V7X_VMEM_BYTES = 64 * 1024 * 1024

N_PIECES = 3
CUM_GROUPS = N_PIECES * N_PIECES
CUM_LANES = V7X_LANES
NEG_BIG = -1e30
LOG2E = 1.4426950408889634
FOX_LOOKAHEAD = 3
FOX_ACC_ROWS = FOX_HEAD_DIM + 16
FFN_CHUNK = 256
N_FFN_CHUNKS = D_FF // FFN_CHUNK


def _seq_tile(seq_len):
    return min(512, seq_len)


def _vmem_limit(nbytes):
    return int(min(V7X_VMEM_BYTES - 4 * 1024 * 1024, nbytes))


def _rms(xf, g):
    return xf * lax.rsqrt(jnp.mean(xf * xf, axis=-1, keepdims=True) + RMS_EPS) * g


def _dot(a, b):
    return jnp.dot(a, b, preferred_element_type=F32)


def _dot_nt(a, b):
    return lax.dot_general(a, b, (((1,), (1,)), ((), ())), preferred_element_type=F32)


def _sigmoid(z):
    return 1.0 / (1.0 + jnp.exp(-z))


def _log_sigmoid(z):
    return jnp.minimum(z, 0.0) - jnp.log(1.0 + jnp.exp(-jnp.abs(z)))


def _bf16_pieces(v):
    p1 = v.astype(BF16)
    r1 = v - p1.astype(F32)
    p2 = r1.astype(BF16)
    r2 = r1 - p2.astype(F32)
    return p1, p2, r2.astype(BF16)


def _memkv_kernel(mem_ref, g_ref, w_ref, kv_ref):
    mem_n = _rms(mem_ref[0], g_ref[0])
    kv_ref[0, 0] = _dot(mem_n.astype(BF16), w_ref[0]).astype(BF16)


def _memkv(mem, mem_g, w_xkv):
    n_layers = w_xkv.shape[0]
    batch = mem.shape[0]
    return pl.pallas_call(
        _memkv_kernel,
        grid=(n_layers, batch),
        in_specs=[
            pl.BlockSpec((1, N_MEM, D_MODEL), lambda l, b: (b, 0, 0)),
            pl.BlockSpec((1, 1, D_MODEL), lambda l, b: (l, 0, 0)),
            pl.BlockSpec((1, D_MODEL, 2 * X_W), lambda l, b: (l, 0, 0)),
        ],
        out_specs=pl.BlockSpec((1, 1, N_MEM, 2 * X_W), lambda l, b: (l, b, 0, 0)),
        out_shape=jax.ShapeDtypeStruct((n_layers, batch, N_MEM, 2 * X_W), BF16),
        compiler_params=pltpu.CompilerParams(dimension_semantics=("arbitrary", "arbitrary")),
        name="mem_kv",
    )(mem, mem_g.reshape(n_layers, 1, D_MODEL), w_xkv)


def _mixin_kernel(x_ref, g_ref, wnat_ref, wf_ref, wvt_ref, bf_ref, pwbd_ref, pscale_ref, wpbr_ref,
                  q_ref, k_ref, vt_ref, kc_ref, gpy_ref, gf_ref, ubuf_ref, ccarry_ref, *, ts):
    si = pl.program_id(1)
    carry_rows = MAX_POOL_WINDOW

    @pl.when(si == 0)
    def _():
        ubuf_ref[0:carry_rows, :] = jnp.zeros((carry_rows, POOL_W), F32)
        ccarry_ref[...] = jnp.zeros_like(ccarry_ref)

    h = _rms(x_ref[0], g_ref[...]).astype(BF16)

    u = _dot(h, wnat_ref[:, 0:POOL_W])
    ubuf_ref[carry_rows:carry_rows + ts, :] = u
    pos = si * ts + lax.broadcasted_iota(jnp.int32, (ts, 1), 0)
    mixed = []
    for pair in range(len(POOL_WINDOWS) // 2):
        pooled_pair = []
        for gi in (2 * pair, 2 * pair + 1):
            w = POOL_WINDOWS[gi]
            cols = pl.ds(gi * POOL_GROUP_DIM, POOL_GROUP_DIM)
            win = ubuf_ref[pl.ds(carry_rows, ts), cols]
            for j in range(1, w):
                win = win + ubuf_ref[pl.ds(carry_rows - j, ts), cols]
            cnt = jnp.minimum(pos + 1, w).astype(F32)
            pooled = win / cnt - ubuf_ref[pl.ds(carry_rows, ts), cols]
            pooled_pair.append(pooled.astype(BF16))
        mixed.append(_dot(jnp.concatenate(pooled_pair, axis=1), pwbd_ref[pair]))
    ubuf_ref[0:carry_rows, :] = ubuf_ref[ts:ts + carry_rows, :]
    mixed = jnp.concatenate(mixed, axis=1) * pscale_ref[...]
    y_pool = _dot(mixed.astype(BF16), wpbr_ref[...])

    gate_pool = _sigmoid(_dot(h, wnat_ref[:, 3 * FOX_W:3 * FOX_W + D_MODEL]))
    gpy_ref[0] = (gate_pool * y_pool).astype(BF16)
    gate_fox = _sigmoid(_dot(h, wnat_ref[:, 3 * FOX_W + D_MODEL:3 * FOX_W + 2 * D_MODEL]))
    gf_ref[0] = gate_fox.astype(BF16)

    q_ref[0] = (_dot(h, wnat_ref[:, OFF_Q:OFF_K]) * (LOG2E / math.sqrt(FOX_HEAD_DIM))).astype(BF16)
    k_ref[0] = _dot(h, wnat_ref[:, OFF_K:OFF_V]).astype(BF16)
    vt_ref[0] = _dot_nt(wvt_ref[...], h).astype(BF16)

    logf = _log_sigmoid(_dot(h, wf_ref[...]) + bf_ref[...]) * LOG2E
    grp = lax.broadcasted_iota(jnp.int32, (1, CUM_LANES), 1) // FOX_HEADS
    live = grp < CUM_GROUPS
    p1, p2, p3 = _bf16_pieces(logf)
    src = grp // N_PIECES
    xin = jnp.where(src == 0, p1, jnp.where(src == 1, p2, p3))
    xin = jnp.where(live, xin, jnp.zeros_like(xin))
    tri = (lax.broadcasted_iota(jnp.int32, (ts, ts), 0)
           >= lax.broadcasted_iota(jnp.int32, (ts, ts), 1)).astype(BF16)
    cum = _dot(tri, xin) + ccarry_ref[0:1, :]
    ccarry_ref[0:1, :] = cum[ts - 1:ts, :]
    s1, s2, s3 = _bf16_pieces(cum)
    sel = grp % N_PIECES
    kc = jnp.where(sel == 0, s1, jnp.where(sel == 1, s2, s3))
    kc_ref[0] = jnp.where(live, kc, jnp.zeros_like(kc))


def _mix_in(x, g_pre, w_nat, w_f, w_vt, b_f, pw_bd, pool_scale, w_pool_br):
    batch, seq, _ = x.shape
    ts = _seq_tile(seq)
    n_nat = w_nat.shape[1]
    const = lambda b, s: (0, 0)
    const3 = lambda b, s: (0, 0, 0)
    one = pl.Buffered(1)
    tok = lambda width: pl.BlockSpec((1, ts, width), lambda b, s: (b, s, 0))
    out_shapes = (
        jax.ShapeDtypeStruct((batch, seq, FOX_W), BF16),
        jax.ShapeDtypeStruct((batch, seq, FOX_W), BF16),
        jax.ShapeDtypeStruct((batch, FOX_W, seq), BF16),
        jax.ShapeDtypeStruct((batch, seq, CUM_LANES), BF16),
        jax.ShapeDtypeStruct((batch, seq, D_MODEL), BF16),
        jax.ShapeDtypeStruct((batch, seq, D_MODEL), BF16),
    )
    return pl.pallas_call(
        functools.partial(_mixin_kernel, ts=ts),
        grid=(batch, seq // ts),
        in_specs=[
            tok(D_MODEL),
            pl.BlockSpec((1, D_MODEL), const),
            pl.BlockSpec((D_MODEL, n_nat), const, pipeline_mode=one),
            pl.BlockSpec((D_MODEL, CUM_LANES), const, pipeline_mode=one),
            pl.BlockSpec((FOX_W, D_MODEL), const, pipeline_mode=one),
            pl.BlockSpec((1, CUM_LANES), const),
            pl.BlockSpec((2, 2 * POOL_GROUP_DIM, 2 * POOL_GROUP_DIM), const3, pipeline_mode=one),
            pl.BlockSpec((1, POOL_W), const),
            pl.BlockSpec((POOL_W, D_MODEL), const, pipeline_mode=one),
        ],
        out_specs=(
            tok(FOX_W), tok(FOX_W),
            pl.BlockSpec((1, FOX_W, ts), lambda b, s: (b, 0, s)),
            tok(CUM_LANES), tok(D_MODEL), tok(D_MODEL),
        ),
        out_shape=out_shapes,
        scratch_shapes=[
            pltpu.VMEM((MAX_POOL_WINDOW + ts, POOL_W), F32),
            pltpu.VMEM((V7X_SUBLANES, CUM_LANES), F32),
        ],
        compiler_params=pltpu.CompilerParams(
            dimension_semantics=("arbitrary", "arbitrary"),
            vmem_limit_bytes=_vmem_limit(56 * 1024 * 1024)),
        name="mix_in",
    )(x, g_pre, w_nat, w_f, w_vt, b_f, pw_bd, pool_scale, w_pool_br)


def _fox_kernel(qi_ref, ki_ref, q_ref, k_ref, kc_ref, vt_ref, o_ref, acc_ref, m_ref, *, t):
    step = pl.program_id(1)
    qi = qi_ref[step]
    ki = ki_ref[step]

    @pl.when(ki == 0)
    def _():
        acc_ref[...] = jnp.zeros_like(acc_ref)
        m_ref[...] = jnp.full_like(m_ref, NEG_BIG)

    lane = lax.broadcasted_iota(jnp.int32, (1, 2 * FOX_HEAD_DIM), 1)
    cum_lane = lax.broadcasted_iota(jnp.int32, (1, CUM_LANES), 1)
    ones_rows = jnp.ones((FOX_ACC_ROWS - FOX_HEAD_DIM, t), BF16)

    def process(masked):
        kcb = kc_ref[0]

        def scores(head):
            pair, a = divmod(head, 2)
            cols = slice(pair * 2 * FOX_HEAD_DIM, (pair + 1) * 2 * FOX_HEAD_DIM)
            lhs = jnp.concatenate([k_ref[0, :, cols], kcb], axis=1)
            qp = q_ref[0, :, cols]
            qa = jnp.where((lane // FOX_HEAD_DIM) == a, qp, jnp.zeros_like(qp))
            pick = jnp.where((cum_lane % FOX_HEADS == head)
                             & (cum_lane // FOX_HEADS < CUM_GROUPS), -1.0, 0.0)
            pick = jnp.broadcast_to(pick, (t, CUM_LANES)).astype(BF16)
            st = _dot_nt(lhs, jnp.concatenate([qa, pick], axis=1))
            if masked:
                causal = (lax.broadcasted_iota(jnp.int32, (t, t), 0)
                          <= lax.broadcasted_iota(jnp.int32, (t, t), 1))
                st = jnp.where(causal, st, NEG_BIG)
            return st

        def softmax_pv(head, st):
            m_old = m_ref[head:head + 1, :]
            m_new = jnp.maximum(m_old, jnp.max(st, axis=0, keepdims=True))
            m_ref[head:head + 1, :] = m_new
            p = jnp.exp2(st - m_new).astype(BF16)
            alpha = jnp.exp2(m_old - m_new)
            vt_aug = jnp.concatenate(
                [vt_ref[0, head * FOX_HEAD_DIM:(head + 1) * FOX_HEAD_DIM, :], ones_rows], axis=0)
            acc_ref[head] = alpha * acc_ref[head] + _dot(vt_aug, p)

        pending = [scores(h) for h in range(FOX_LOOKAHEAD)]
        for head in range(FOX_HEADS):
            if head + FOX_LOOKAHEAD < FOX_HEADS:
                pending.append(scores(head + FOX_LOOKAHEAD))
            softmax_pv(head, pending.pop(0))

    @pl.when(ki < qi)
    def _():
        process(False)

    @pl.when(ki == qi)
    def _():
        process(True)
        outs = []
        for head in range(FOX_HEADS):
            inv_l = 1.0 / acc_ref[head, FOX_HEAD_DIM:FOX_HEAD_DIM + 1, :]
            outs.append(acc_ref[head, 0:FOX_HEAD_DIM, :] * inv_l)
        o_ref[0] = jnp.concatenate(outs, axis=0).T.astype(BF16)


def _fox_attention(q, k, kc, vt):
    batch, seq, _ = q.shape
    t = _seq_tile(seq)
    nq = seq // t
    qi_tab = np.concatenate([np.full(i + 1, i, np.int32) for i in range(nq)])
    ki_tab = np.concatenate([np.arange(i + 1, dtype=np.int32) for i in range(nq)])
    grid_spec = pltpu.PrefetchScalarGridSpec(
        num_scalar_prefetch=2,
        grid=(batch, int(qi_tab.shape[0])),
        in_specs=[
            pl.BlockSpec((1, t, FOX_W), lambda b, s, qi, ki: (b, qi[s], 0)),
            pl.BlockSpec((1, t, FOX_W), lambda b, s, qi, ki: (b, ki[s], 0)),
            pl.BlockSpec((1, t, CUM_LANES), lambda b, s, qi, ki: (b, ki[s], 0)),
            pl.BlockSpec((1, FOX_W, t), lambda b, s, qi, ki: (b, 0, ki[s])),
        ],
        out_specs=pl.BlockSpec((1, t, FOX_W), lambda b, s, qi, ki: (b, qi[s], 0)),
        scratch_shapes=[
            pltpu.VMEM((FOX_HEADS, FOX_ACC_ROWS, t), F32),
            pltpu.VMEM((FOX_HEADS, t), F32),
        ],
    )
    return pl.pallas_call(
        functools.partial(_fox_kernel, t=t),
        grid_spec=grid_spec,
        out_shape=jax.ShapeDtypeStruct((batch, seq, FOX_W), BF16),
        compiler_params=pltpu.CompilerParams(
            dimension_semantics=("arbitrary", "arbitrary"),
            vmem_limit_bytes=_vmem_limit(56 * 1024 * 1024)),
        name="fox_attention",
    )(jnp.asarray(qi_tab), jnp.asarray(ki_tab), q, k, kc, vt)


def _mixout_kernel(o_ref, gpy_ref, gf_ref, x_ref, kv_ref, wfbr_ref, wmo_ref, wxq_ref, wxo_ref,
                   g_post_ref, g_xpre_ref, g_xpost_ref, out_ref):
    y_fox = _dot(o_ref[0], wfbr_ref[...])
    merged = gpy_ref[0].astype(F32) + gf_ref[0].astype(F32) * y_fox
    x1 = x_ref[0] + _rms(_dot(merged.astype(BF16), wmo_ref[...]), g_post_ref[...])

    h = _rms(x1, g_xpre_ref[...]).astype(BF16)
    qx = _dot(h, wxq_ref[...]).astype(BF16)
    kv = kv_ref[0, 0]
    heads = []
    for head in range(X_HEADS):
        cols = slice(head * X_HEAD_DIM, (head + 1) * X_HEAD_DIM)
        vcols = slice(X_W + head * X_HEAD_DIM, X_W + (head + 1) * X_HEAD_DIM)
        s = _dot_nt(qx[:, cols], kv[:, cols]) * (1.0 / math.sqrt(X_HEAD_DIM))
        p = jnp.exp(s - jnp.max(s, axis=-1, keepdims=True))
        inv_l = 1.0 / jnp.sum(p, axis=-1, keepdims=True)
        heads.append((_dot(p.astype(BF16), kv[:, vcols]) * inv_l).astype(BF16))
    xo = _dot(jnp.concatenate(heads, axis=1), wxo_ref[...])
    out_ref[0] = x1 + _rms(xo, g_xpost_ref[...])


def _mix_out(o, gpy, gf, x, kv_all, layer, w_fox_br, w_mix_out, w_xq, w_xo, g_post, g_xpre, g_xpost):
    batch, seq, _ = x.shape
    ts = _seq_tile(seq)
    const = lambda b, s: (0, 0)
    one = pl.Buffered(1)
    tok = lambda width: pl.BlockSpec((1, ts, width), lambda b, s: (b, s, 0))
    return pl.pallas_call(
        _mixout_kernel,
        grid=(batch, seq // ts),
        in_specs=[
            tok(FOX_W), tok(D_MODEL), tok(D_MODEL), tok(D_MODEL),
            pl.BlockSpec((1, 1, N_MEM, 2 * X_W), lambda b, s: (layer, b, 0, 0)),
            pl.BlockSpec((FOX_W, D_MODEL), const, pipeline_mode=one),
            pl.BlockSpec((D_MODEL, D_MODEL), const, pipeline_mode=one),
            pl.BlockSpec((D_MODEL, X_W), const, pipeline_mode=one),
            pl.BlockSpec((X_W, D_MODEL), const, pipeline_mode=one),
            pl.BlockSpec((1, D_MODEL), const),
            pl.BlockSpec((1, D_MODEL), const),
            pl.BlockSpec((1, D_MODEL), const),
        ],
        out_specs=tok(D_MODEL),
        out_shape=jax.ShapeDtypeStruct((batch, seq, D_MODEL), F32),
        compiler_params=pltpu.CompilerParams(
            dimension_semantics=("arbitrary", "arbitrary"),
            vmem_limit_bytes=_vmem_limit(56 * 1024 * 1024)),
        name="mix_out_xattn",
    )(o, gpy, gf, x, kv_all, w_fox_br, w_mix_out, w_xq, w_xo, g_post, g_xpre, g_xpost)


def _ffn_kernel(x_ref, g_pre_ref, g_post_ref, wg_ref, wu_ref, cwg_ref, cwu_ref, cbg_ref, cbu_ref,
                wd_ref, out_ref, carry_g_ref, carry_u_ref, stage_ref, act_ref, *, ts):
    si = pl.program_id(1)
    pad = V7X_SUBLANES

    @pl.when(si == 0)
    def _():
        carry_g_ref[...] = jnp.zeros_like(carry_g_ref)
        carry_u_ref[...] = jnp.zeros_like(carry_u_ref)

    x = x_ref[0]
    h = _rms(x, g_pre_ref[...]).astype(BF16)

    def conv(z, carry_ref, cw_ref, cb_ref, c, slot):
        stage_ref[slot, 0:pad, :] = carry_ref[c]
        stage_ref[slot, pad:pad + ts, :] = z
        carry_ref[c] = z[ts - pad:ts, :]
        z1 = stage_ref[slot, pl.ds(pad - 1, ts), :]
        z2 = stage_ref[slot, pl.ds(pad - 2, ts), :]
        cw = cw_ref[c]
        return cw[2:3, :] * z + cw[1:2, :] * z1 + cw[0:1, :] * z2 + cb_ref[c]

    for c in range(N_FFN_CHUNKS):
        zg = conv(_dot(h, wg_ref[c]), carry_g_ref, cwg_ref, cbg_ref, c, 0)
        zu = conv(_dot(h, wu_ref[c]), carry_u_ref, cwu_ref, cbu_ref, c, 1)
        act = jax.nn.gelu(zg, approximate=True) * zu
        act_ref[:, c * FFN_CHUNK:(c + 1) * FFN_CHUNK] = act.astype(BF16)

    y = _dot(act_ref[...], wd_ref[...])
    out_ref[0] = x + _rms(y, g_post_ref[...])


def _conv_ffn(x, g_pre, g_post, w_gate, w_up, cw_g, cw_u, cb_g, cb_u, w_down):
    batch, seq, _ = x.shape
    ts = _seq_tile(seq)
    const = lambda b, s: (0, 0)
    const3 = lambda b, s: (0, 0, 0)
    one = pl.Buffered(1)
    tok = pl.BlockSpec((1, ts, D_MODEL), lambda b, s: (b, s, 0))
    chunked = lambda rows, cols: pl.BlockSpec((N_FFN_CHUNKS, rows, cols), const3, pipeline_mode=one)
    return pl.pallas_call(
        functools.partial(_ffn_kernel, ts=ts),
        grid=(batch, seq // ts),
        in_specs=[
            tok,
            pl.BlockSpec((1, D_MODEL), const),
            pl.BlockSpec((1, D_MODEL), const),
            chunked(D_MODEL, FFN_CHUNK), chunked(D_MODEL, FFN_CHUNK),
            chunked(3, FFN_CHUNK), chunked(3, FFN_CHUNK),
            chunked(1, FFN_CHUNK), chunked(1, FFN_CHUNK),
            pl.BlockSpec((D_FF, D_MODEL), const, pipeline_mode=one),
        ],
        out_specs=tok,
        out_shape=jax.ShapeDtypeStruct((batch, seq, D_MODEL), F32),
        scratch_shapes=[
            pltpu.VMEM((N_FFN_CHUNKS, V7X_SUBLANES, FFN_CHUNK), F32),
            pltpu.VMEM((N_FFN_CHUNKS, V7X_SUBLANES, FFN_CHUNK), F32),
            pltpu.VMEM((2, V7X_SUBLANES + ts, FFN_CHUNK), F32),
            pltpu.VMEM((ts, D_FF), BF16),
        ],
        compiler_params=pltpu.CompilerParams(
            dimension_semantics=("arbitrary", "arbitrary"),
            vmem_limit_bytes=_vmem_limit(56 * 1024 * 1024)),
        name="conv_ffn",
    )(x, g_pre, g_post, w_gate, w_up, cw_g, cw_u, cb_g, cb_u, w_down)


def _chunk_cols(w):
    rows = w.shape[0]
    return w.reshape(rows, N_FFN_CHUNKS, FFN_CHUNK).transpose(1, 0, 2)


def _cum_lane_layout(w):
    rep = jnp.concatenate([w] * CUM_GROUPS, axis=-1)
    pad = [(0, 0)] * (w.ndim - 1) + [(0, CUM_LANES - CUM_GROUPS * FOX_HEADS)]
    return jnp.pad(rep, pad)


def _block_diag_pairs(pw):
    z = jnp.zeros((POOL_GROUP_DIM, POOL_GROUP_DIM), pw.dtype)
    return jnp.stack([
        jnp.concatenate([jnp.concatenate([pw[2 * i], z], axis=1),
                         jnp.concatenate([z, pw[2 * i + 1]], axis=1)], axis=0)
        for i in range(len(POOL_WINDOWS) // 2)])


def kernel(x, mem, mix_pre_g, mix_post_g, w_in, b_forget, pool_w, pool_scale, w_pool_br, w_fox_br,
           w_mix_out, xa_pre_g, xa_post_g, mem_g, w_xq, w_xkv, w_xo, ffn_pre_g, ffn_post_g, w_up,
           conv_w, conv_b, w_down):
    n_layers = w_in.shape[0]
    row = lambda v: v.reshape(1, -1)
    kv_all = _memkv(mem, mem_g, w_xkv.astype(BF16))
    for l in range(n_layers):
        w_in_l = w_in[l]
        w_nat = jnp.concatenate([w_in_l[:, :OFF_V], w_in_l[:, OFF_GP:]], axis=1).astype(BF16)
        w_f = _cum_lane_layout(w_in_l[:, OFF_F:OFF_GP]).astype(BF16)
        w_vt = w_in_l[:, OFF_V:OFF_F].T.astype(BF16)
        b_f = _cum_lane_layout(b_forget[l]).reshape(1, CUM_LANES)
        q, k, vt, kc, gpy, gf = _mix_in(
            x, row(mix_pre_g[l]), w_nat, w_f, w_vt, b_f, _block_diag_pairs(pool_w[l]).astype(BF16),
            row(pool_scale[l]), w_pool_br[l].astype(BF16))
        o = _fox_attention(q, k, kc, vt)
        x = _mix_out(o, gpy, gf, x, kv_all, l, w_fox_br[l].astype(BF16), w_mix_out[l].astype(BF16),
                     w_xq[l].astype(BF16), w_xo[l].astype(BF16), row(mix_post_g[l]),
                     row(xa_pre_g[l]), row(xa_post_g[l]))
        x = _conv_ffn(
            x, row(ffn_pre_g[l]), row(ffn_post_g[l]),
            _chunk_cols(w_up[l][:, :D_FF]).astype(BF16), _chunk_cols(w_up[l][:, D_FF:]).astype(BF16),
            _chunk_cols(conv_w[l][:, :D_FF]), _chunk_cols(conv_w[l][:, D_FF:]),
            _chunk_cols(conv_b[l][None, :D_FF]), _chunk_cols(conv_b[l][None, D_FF:]),
            w_down[l].astype(BF16))
    return x
```

```python
import functools
import math

import jax
import jax.numpy as jnp
import numpy as np
from jax import lax
from jax.experimental import pallas as pl
from jax.experimental.pallas import tpu as pltpu

F32 = jnp.float32
BF16 = jnp.bfloat16

D_MODEL = 1024
POOL_WINDOWS = (2, 4, 8, 16)
POOL_GROUP_DIM = 128
POOL_W = len(POOL_WINDOWS) * POOL_GROUP_DIM
MAX_POOL_WINDOW = max(POOL_WINDOWS)
FOX_HEADS = 8
FOX_HEAD_DIM = 64
FOX_W = FOX_HEADS * FOX_HEAD_DIM
OFF_Q = POOL_W
OFF_K = OFF_Q + FOX_W
OFF_V = OFF_K + FOX_W
OFF_F = OFF_V + FOX_W
OFF_GP = OFF_F + FOX_HEADS
OFF_GF = OFF_GP + D_MODEL
N_MEM = 256
X_HEADS = 4
X_HEAD_DIM = 128
X_W = X_HEADS * X_HEAD_DIM
D_FF = 2816
RMS_EPS = 1e-6

V7X_LANES = 128
V7X_SUBLANES = 8
V7X_VMEM_BYTES = 64 * 1024 * 1024

N_PIECES = 3
CUM_GROUPS = N_PIECES * N_PIECES
CUM_LANES = V7X_LANES
NEG_BIG = -1e30
LOG2E = 1.4426950408889634
FOX_LOOKAHEAD = 3
FOX_KV_TILES_PER_STEP = 4
FOX_ACC_ROWS = FOX_HEAD_DIM + 16
FFN_CHUNK = 256
N_FFN_CHUNKS = D_FF // FFN_CHUNK


def _seq_tile(seq_len):
    return min(512, seq_len)


def _vmem_limit(nbytes):
    return int(min(V7X_VMEM_BYTES - 4 * 1024 * 1024, nbytes))


def _rms(xf, g):
    return xf * lax.rsqrt(jnp.mean(xf * xf, axis=-1, keepdims=True) + RMS_EPS) * g


def _dot(a, b):
    return jnp.dot(a, b, preferred_element_type=F32)


def _dot_nt(a, b):
    return lax.dot_general(a, b, (((1,), (1,)), ((), ())), preferred_element_type=F32)


def _sigmoid(z):
    return 1.0 / (1.0 + jnp.exp(-z))


def _log_sigmoid(z):
    return jnp.minimum(z, 0.0) - jnp.log(1.0 + jnp.exp(-jnp.abs(z)))


def _bf16_pieces(v):
    p1 = v.astype(BF16)
    r1 = v - p1.astype(F32)
    p2 = r1.astype(BF16)
    r2 = r1 - p2.astype(F32)
    return p1, p2, r2.astype(BF16)


def _memkv_kernel(mem_ref, g_ref, w_ref, kv_ref):
    mem_n = _rms(mem_ref[0], g_ref[0])
    kv_ref[0, 0] = _dot(mem_n.astype(BF16), w_ref[0]).astype(BF16)


def _memkv(mem, mem_g, w_xkv):
    n_layers = w_xkv.shape[0]
    batch = mem.shape[0]
    return pl.pallas_call(
        _memkv_kernel,
        grid=(n_layers, batch),
        in_specs=[
            pl.BlockSpec((1, N_MEM, D_MODEL), lambda l, b: (b, 0, 0)),
            pl.BlockSpec((1, 1, D_MODEL), lambda l, b: (l, 0, 0)),
            pl.BlockSpec((1, D_MODEL, 2 * X_W), lambda l, b: (l, 0, 0)),
        ],
        out_specs=pl.BlockSpec((1, 1, N_MEM, 2 * X_W), lambda l, b: (l, b, 0, 0)),
        out_shape=jax.ShapeDtypeStruct((n_layers, batch, N_MEM, 2 * X_W), BF16),
        compiler_params=pltpu.CompilerParams(dimension_semantics=("arbitrary", "arbitrary")),
        name="mem_kv",
    )(mem, mem_g.reshape(n_layers, 1, D_MODEL), w_xkv)


def _mixin_kernel(x_ref, g_ref, wnat_ref, wf_ref, wvt_ref, bf_ref, pwbd_ref, pscale_ref, wpbr_ref,
                  q_ref, k_ref, vt_ref, kc_ref, gpy_ref, gf_ref, ubuf_ref, ccarry_ref, *, ts):
    si = pl.program_id(1)
    carry_rows = MAX_POOL_WINDOW

    @pl.when(si == 0)
    def _():
        ubuf_ref[0:carry_rows, :] = jnp.zeros((carry_rows, POOL_W), F32)
        ccarry_ref[...] = jnp.zeros_like(ccarry_ref)

    h = _rms(x_ref[0], g_ref[...]).astype(BF16)

    u = _dot(h, wnat_ref[:, 0:POOL_W])
    ubuf_ref[carry_rows:carry_rows + ts, :] = u
    q_ref[0] = (_dot(h, wnat_ref[:, OFF_Q:OFF_K]) * (LOG2E / math.sqrt(FOX_HEAD_DIM))).astype(BF16)
    k_ref[0] = _dot(h, wnat_ref[:, OFF_K:OFF_V]).astype(BF16)
    vt_ref[0, 0] = _dot_nt(wvt_ref[...], h).astype(BF16)

    pos = si * ts + lax.broadcasted_iota(jnp.int32, (ts, 1), 0)
    mixed = []
    for pair in range(len(POOL_WINDOWS) // 2):
        pooled_pair = []
        for gi in (2 * pair, 2 * pair + 1):
            w = POOL_WINDOWS[gi]
            cols = pl.ds(gi * POOL_GROUP_DIM, POOL_GROUP_DIM)
            win = ubuf_ref[pl.ds(carry_rows, ts), cols]
            for j in range(1, w):
                win = win + ubuf_ref[pl.ds(carry_rows - j, ts), cols]
            cnt = jnp.minimum(pos + 1, w).astype(F32)
            pooled = win / cnt - ubuf_ref[pl.ds(carry_rows, ts), cols]
            pooled_pair.append(pooled.astype(BF16))
        mixed.append(_dot(jnp.concatenate(pooled_pair, axis=1), pwbd_ref[pair]))
    ubuf_ref[0:carry_rows, :] = ubuf_ref[ts:ts + carry_rows, :]
    mixed = jnp.concatenate(mixed, axis=1) * pscale_ref[...]
    y_pool = _dot(mixed.astype(BF16), wpbr_ref[...])

    gate_pool = _sigmoid(_dot(h, wnat_ref[:, 3 * FOX_W:3 * FOX_W + D_MODEL]))
    gpy_ref[0] = (gate_pool * y_pool).astype(BF16)
    gate_fox = _sigmoid(_dot(h, wnat_ref[:, 3 * FOX_W + D_MODEL:3 * FOX_W + 2 * D_MODEL]))
    gf_ref[0] = gate_fox.astype(BF16)

    logf = _log_sigmoid(_dot(h, wf_ref[...]) + bf_ref[...]) * LOG2E
    grp = lax.broadcasted_iota(jnp.int32, (1, CUM_LANES), 1) // FOX_HEADS
    live = grp < CUM_GROUPS
    p1, p2, p3 = _bf16_pieces(logf)
    src = grp // N_PIECES
    xin = jnp.where(src == 0, p1, jnp.where(src == 1, p2, p3))
    xin = jnp.where(live, xin, jnp.zeros_like(xin))
    tri = (lax.broadcasted_iota(jnp.int32, (ts, ts), 0)
           >= lax.broadcasted_iota(jnp.int32, (ts, ts), 1)).astype(BF16)
    cum = _dot(tri, xin) + ccarry_ref[0:1, :]
    ccarry_ref[0:1, :] = cum[ts - 1:ts, :]
    s1, s2, s3 = _bf16_pieces(cum)
    sel = grp % N_PIECES
    kc = jnp.where(sel == 0, s1, jnp.where(sel == 1, s2, s3))
    kc_ref[0] = jnp.where(live, kc, jnp.zeros_like(kc))


def _mix_in(x, g_pre, w_nat, w_f, w_vt, b_f, pw_bd, pool_scale, w_pool_br):
    batch, seq, _ = x.shape
    ts = _seq_tile(seq)
    n_nat = w_nat.shape[1]
    const = lambda b, s: (0, 0)
    const3 = lambda b, s: (0, 0, 0)
    one = pl.Buffered(1)
    tok = lambda width: pl.BlockSpec((1, ts, width), lambda b, s: (b, s, 0))
    out_shapes = (
        jax.ShapeDtypeStruct((batch, seq, FOX_W), BF16),
        jax.ShapeDtypeStruct((batch, seq, FOX_W), BF16),
        jax.ShapeDtypeStruct((batch, seq // ts, FOX_W, ts), BF16),
        jax.ShapeDtypeStruct((batch, seq, CUM_LANES), BF16),
        jax.ShapeDtypeStruct((batch, seq, D_MODEL), BF16),
        jax.ShapeDtypeStruct((batch, seq, D_MODEL), BF16),
    )
    return pl.pallas_call(
        functools.partial(_mixin_kernel, ts=ts),
        grid=(batch, seq // ts),
        in_specs=[
            tok(D_MODEL),
            pl.BlockSpec((1, D_MODEL), const),
            pl.BlockSpec((D_MODEL, n_nat), const, pipeline_mode=one),
            pl.BlockSpec((D_MODEL, CUM_LANES), const, pipeline_mode=one),
            pl.BlockSpec((FOX_W, D_MODEL), const, pipeline_mode=one),
            pl.BlockSpec((1, CUM_LANES), const),
            pl.BlockSpec((2, 2 * POOL_GROUP_DIM, 2 * POOL_GROUP_DIM), const3, pipeline_mode=one),
            pl.BlockSpec((1, POOL_W), const),
            pl.BlockSpec((POOL_W, D_MODEL), const, pipeline_mode=one),
        ],
        out_specs=(
            tok(FOX_W), tok(FOX_W),
            pl.BlockSpec((1, 1, FOX_W, ts), lambda b, s: (b, s, 0, 0)),
            tok(CUM_LANES), tok(D_MODEL), tok(D_MODEL),
        ),
        out_shape=out_shapes,
        scratch_shapes=[
            pltpu.VMEM((MAX_POOL_WINDOW + ts, POOL_W), F32),
            pltpu.VMEM((V7X_SUBLANES, CUM_LANES), F32),
        ],
        compiler_params=pltpu.CompilerParams(
            dimension_semantics=("arbitrary", "arbitrary"),
            vmem_limit_bytes=_vmem_limit(56 * 1024 * 1024)),
        name="mix_in",
    )(x, g_pre, w_nat, w_f, w_vt, b_f, pw_bd, pool_scale, w_pool_br)


def _fox_kernel(qi_ref, kb_ref, q_ref, k_ref, kc_ref, vt_ref, o_ref, acc_ref, m_ref, *, t, kv_tiles):
    step = pl.program_id(1)
    qi = qi_ref[step]
    kb = kb_ref[step]

    @pl.when(kb == 0)
    def _():
        acc_ref[...] = jnp.zeros_like(acc_ref)
        m_ref[...] = jnp.full_like(m_ref, NEG_BIG)

    lane = lax.broadcasted_iota(jnp.int32, (1, 2 * FOX_HEAD_DIM), 1)
    cum_lane = lax.broadcasted_iota(jnp.int32, (1, CUM_LANES), 1)
    ones_rows = jnp.ones((FOX_ACC_ROWS - FOX_HEAD_DIM, t), BF16)

    def process(j, masked):
        kcb = kc_ref[0, j]

        def scores(head):
            pair, a = divmod(head, 2)
            cols = slice(pair * 2 * FOX_HEAD_DIM, (pair + 1) * 2 * FOX_HEAD_DIM)
            lhs = jnp.concatenate([k_ref[0, j, :, cols], kcb], axis=1)
            qp = q_ref[0, :, cols]
            qa = jnp.where((lane // FOX_HEAD_DIM) == a, qp, jnp.zeros_like(qp))
            pick = jnp.where((cum_lane % FOX_HEADS == head)
                             & (cum_lane // FOX_HEADS < CUM_GROUPS), -1.0, 0.0)
            pick = jnp.broadcast_to(pick, (t, CUM_LANES)).astype(BF16)
            st = _dot_nt(lhs, jnp.concatenate([qa, pick], axis=1))
            if masked:
                causal = (lax.broadcasted_iota(jnp.int32, (t, t), 0)
                          <= lax.broadcasted_iota(jnp.int32, (t, t), 1))
                st = jnp.where(causal, st, NEG_BIG)
            return st

        def softmax_pv(head, st):
            m_old = m_ref[head:head + 1, :]
            m_new = jnp.maximum(m_old, jnp.max(st, axis=0, keepdims=True))
            m_ref[head:head + 1, :] = m_new
            p = jnp.exp2(st - m_new).astype(BF16)
            alpha = jnp.exp2(m_old - m_new)
            vt_aug = jnp.concatenate(
                [vt_ref[0, j, head * FOX_HEAD_DIM:(head + 1) * FOX_HEAD_DIM, :], ones_rows], axis=0)
            acc_ref[head] = alpha * acc_ref[head] + _dot(vt_aug, p)

        pending = [scores(h) for h in range(FOX_LOOKAHEAD)]
        for head in range(FOX_HEADS):
            if head + FOX_LOOKAHEAD < FOX_HEADS:
                pending.append(scores(head + FOX_LOOKAHEAD))
            softmax_pv(head, pending.pop(0))

    def key_tile(j, carry):
        ki = kb * kv_tiles + j

        @pl.when(ki < qi)
        def _():
            process(j, False)

        @pl.when(ki == qi)
        def _():
            process(j, True)

        return carry

    lax.fori_loop(0, kv_tiles, key_tile, 0)

    @pl.when(kb == qi // kv_tiles)
    def _():
        outs = []
        for head in range(FOX_HEADS):
            inv_l = 1.0 / acc_ref[head, FOX_HEAD_DIM:FOX_HEAD_DIM + 1, :]
            outs.append(acc_ref[head, 0:FOX_HEAD_DIM, :] * inv_l)
        o_ref[0] = jnp.concatenate(outs, axis=0).T.astype(BF16)


def _fox_attention(q, k, kc, vt):
    batch, seq, _ = q.shape
    t = _seq_tile(seq)
    nq = seq // t
    kv_tiles = math.gcd(FOX_KV_TILES_PER_STEP, nq)
    qi_tab = np.concatenate([np.full(i // kv_tiles + 1, i, np.int32) for i in range(nq)])
    kb_tab = np.concatenate([np.arange(i // kv_tiles + 1, dtype=np.int32) for i in range(nq)])
    grid_spec = pltpu.PrefetchScalarGridSpec(
        num_scalar_prefetch=2,
        grid=(batch, int(qi_tab.shape[0])),
        in_specs=[
            pl.BlockSpec((1, t, FOX_W), lambda b, s, qi, kb: (b, qi[s], 0)),
            pl.BlockSpec((1, kv_tiles, t, FOX_W), lambda b, s, qi, kb: (b, kb[s], 0, 0)),
            pl.BlockSpec((1, kv_tiles, t, CUM_LANES), lambda b, s, qi, kb: (b, kb[s], 0, 0)),
            pl.BlockSpec((1, kv_tiles, FOX_W, t), lambda b, s, qi, kb: (b, kb[s], 0, 0)),
        ],
        out_specs=pl.BlockSpec((1, t, FOX_W), lambda b, s, qi, kb: (b, qi[s], 0)),
        scratch_shapes=[
            pltpu.VMEM((FOX_HEADS, FOX_ACC_ROWS, t), F32),
            pltpu.VMEM((FOX_HEADS, t), F32),
        ],
    )
    return pl.pallas_call(
        functools.partial(_fox_kernel, t=t, kv_tiles=kv_tiles),
        grid_spec=grid_spec,
        out_shape=jax.ShapeDtypeStruct((batch, seq, FOX_W), BF16),
        compiler_params=pltpu.CompilerParams(
            dimension_semantics=("arbitrary", "arbitrary"),
            vmem_limit_bytes=_vmem_limit(56 * 1024 * 1024)),
        name="fox_attention",
    )(jnp.asarray(qi_tab), jnp.asarray(kb_tab), q,
      k.reshape(batch, nq, t, FOX_W), kc.reshape(batch, nq, t, CUM_LANES), vt)


def _mixout_kernel(o_ref, gpy_ref, gf_ref, x_ref, kv_ref, wfbr_ref, wmo_ref, wxq_ref, wxo_ref,
                   g_post_ref, g_xpre_ref, g_xpost_ref, out_ref):
    y_fox = _dot(o_ref[0], wfbr_ref[...])
    merged = gpy_ref[0].astype(F32) + gf_ref[0].astype(F32) * y_fox
    x1 = x_ref[0] + _rms(_dot(merged.astype(BF16), wmo_ref[...]), g_post_ref[...])

    h = _rms(x1, g_xpre_ref[...]).astype(BF16)
    qx = _dot(h, wxq_ref[...]).astype(BF16)
    kv = kv_ref[0, 0]
    heads = []
    for head in range(X_HEADS):
        cols = slice(head * X_HEAD_DIM, (head + 1) * X_HEAD_DIM)
        vcols = slice(X_W + head * X_HEAD_DIM, X_W + (head + 1) * X_HEAD_DIM)
        s = _dot_nt(qx[:, cols], kv[:, cols]) * (1.0 / math.sqrt(X_HEAD_DIM))
        p = jnp.exp(s - jnp.max(s, axis=-1, keepdims=True))
        inv_l = 1.0 / jnp.sum(p, axis=-1, keepdims=True)
        heads.append((_dot(p.astype(BF16), kv[:, vcols]) * inv_l).astype(BF16))
    xo = _dot(jnp.concatenate(heads, axis=1), wxo_ref[...])
    out_ref[0] = x1 + _rms(xo, g_xpost_ref[...])


def _mix_out(o, gpy, gf, x, kv_all, layer, w_fox_br, w_mix_out, w_xq, w_xo, g_post, g_xpre, g_xpost):
    batch, seq, _ = x.shape
    ts = _seq_tile(seq)
    const = lambda b, s: (0, 0)
    one = pl.Buffered(1)
    tok = lambda width: pl.BlockSpec((1, ts, width), lambda b, s: (b, s, 0))
    return pl.pallas_call(
        _mixout_kernel,
        grid=(batch, seq // ts),
        in_specs=[
            tok(FOX_W), tok(D_MODEL), tok(D_MODEL), tok(D_MODEL),
            pl.BlockSpec((1, 1, N_MEM, 2 * X_W), lambda b, s: (layer, b, 0, 0)),
            pl.BlockSpec((FOX_W, D_MODEL), const, pipeline_mode=one),
            pl.BlockSpec((D_MODEL, D_MODEL), const, pipeline_mode=one),
            pl.BlockSpec((D_MODEL, X_W), const, pipeline_mode=one),
            pl.BlockSpec((X_W, D_MODEL), const, pipeline_mode=one),
            pl.BlockSpec((1, D_MODEL), const),
            pl.BlockSpec((1, D_MODEL), const),
            pl.BlockSpec((1, D_MODEL), const),
        ],
        out_specs=tok(D_MODEL),
        out_shape=jax.ShapeDtypeStruct((batch, seq, D_MODEL), F32),
        compiler_params=pltpu.CompilerParams(
            dimension_semantics=("arbitrary", "arbitrary"),
            vmem_limit_bytes=_vmem_limit(56 * 1024 * 1024)),
        name="mix_out_xattn",
    )(o, gpy, gf, x, kv_all, w_fox_br, w_mix_out, w_xq, w_xo, g_post, g_xpre, g_xpost)


def _ffn_kernel(x_ref, g_pre_ref, g_post_ref, wg_ref, wu_ref, cwg_ref, cwu_ref, cbg_ref, cbu_ref,
                wd_ref, out_ref, carry_g_ref, carry_u_ref, stage_ref, act_ref, *, ts):
    si = pl.program_id(1)
    pad = V7X_SUBLANES

    @pl.when(si == 0)
    def _():
        carry_g_ref[...] = jnp.zeros_like(carry_g_ref)
        carry_u_ref[...] = jnp.zeros_like(carry_u_ref)

    x = x_ref[0]
    h = _rms(x, g_pre_ref[...]).astype(BF16)

    def conv(z, carry_ref, cw_ref, cb_ref, c, slot):
        stage_ref[slot, 0:pad, :] = carry_ref[c]
        stage_ref[slot, pad:pad + ts, :] = z
        carry_ref[c] = z[ts - pad:ts, :]
        z1 = stage_ref[slot, pl.ds(pad - 1, ts), :]
        z2 = stage_ref[slot, pl.ds(pad - 2, ts), :]
        cw = cw_ref[c]
        return cw[2:3, :] * z + cw[1:2, :] * z1 + cw[0:1, :] * z2 + cb_ref[c]

    for c in range(N_FFN_CHUNKS):
        zg = conv(_dot(h, wg_ref[c]), carry_g_ref, cwg_ref, cbg_ref, c, 0)
        zu = conv(_dot(h, wu_ref[c]), carry_u_ref, cwu_ref, cbu_ref, c, 1)
        act = jax.nn.gelu(zg, approximate=True) * zu
        act_ref[:, c * FFN_CHUNK:(c + 1) * FFN_CHUNK] = act.astype(BF16)

    y = _dot(act_ref[...], wd_ref[...])
    out_ref[0] = x + _rms(y, g_post_ref[...])


def _conv_ffn(x, g_pre, g_post, w_gate, w_up, cw_g, cw_u, cb_g, cb_u, w_down):
    batch, seq, _ = x.shape
    ts = _seq_tile(seq)
    const = lambda b, s: (0, 0)
    const3 = lambda b, s: (0, 0, 0)
    one = pl.Buffered(1)
    tok = pl.BlockSpec((1, ts, D_MODEL), lambda b, s: (b, s, 0))
    chunked = lambda rows, cols: pl.BlockSpec((N_FFN_CHUNKS, rows, cols), const3, pipeline_mode=one)
    return pl.pallas_call(
        functools.partial(_ffn_kernel, ts=ts),
        grid=(batch, seq // ts),
        in_specs=[
            tok,
            pl.BlockSpec((1, D_MODEL), const),
            pl.BlockSpec((1, D_MODEL), const),
            chunked(D_MODEL, FFN_CHUNK), chunked(D_MODEL, FFN_CHUNK),
            chunked(3, FFN_CHUNK), chunked(3, FFN_CHUNK),
            chunked(1, FFN_CHUNK), chunked(1, FFN_CHUNK),
            pl.BlockSpec((D_FF, D_MODEL), const, pipeline_mode=one),
        ],
        out_specs=tok,
        out_shape=jax.ShapeDtypeStruct((batch, seq, D_MODEL), F32),
        scratch_shapes=[
            pltpu.VMEM((N_FFN_CHUNKS, V7X_SUBLANES, FFN_CHUNK), F32),
            pltpu.VMEM((N_FFN_CHUNKS, V7X_SUBLANES, FFN_CHUNK), F32),
            pltpu.VMEM((2, V7X_SUBLANES + ts, FFN_CHUNK), F32),
            pltpu.VMEM((ts, D_FF), BF16),
        ],
        compiler_params=pltpu.CompilerParams(
            dimension_semantics=("arbitrary", "arbitrary"),
            vmem_limit_bytes=_vmem_limit(56 * 1024 * 1024)),
        name="conv_ffn",
    )(x, g_pre, g_post, w_gate, w_up, cw_g, cw_u, cb_g, cb_u, w_down)


def _chunk_cols(w):
    rows = w.shape[0]
    return w.reshape(rows, N_FFN_CHUNKS, FFN_CHUNK).transpose(1, 0, 2)


def _cum_lane_layout(w):
    rep = jnp.concatenate([w] * CUM_GROUPS, axis=-1)
    pad = [(0, 0)] * (w.ndim - 1) + [(0, CUM_LANES - CUM_GROUPS * FOX_HEADS)]
    return jnp.pad(rep, pad)


def _block_diag_pairs(pw):
    z = jnp.zeros((POOL_GROUP_DIM, POOL_GROUP_DIM), pw.dtype)
    return jnp.stack([
        jnp.concatenate([jnp.concatenate([pw[2 * i], z], axis=1),
                         jnp.concatenate([z, pw[2 * i + 1]], axis=1)], axis=0)
        for i in range(len(POOL_WINDOWS) // 2)])


def kernel(x, mem, mix_pre_g, mix_post_g, w_in, b_forget, pool_w, pool_scale, w_pool_br, w_fox_br,
           w_mix_out, xa_pre_g, xa_post_g, mem_g, w_xq, w_xkv, w_xo, ffn_pre_g, ffn_post_g, w_up,
           conv_w, conv_b, w_down):
    n_layers = w_in.shape[0]
    row = lambda v: v.reshape(1, -1)
    kv_all = _memkv(mem, mem_g, w_xkv.astype(BF16))
    for l in range(n_layers):
        w_in_l = w_in[l]
        w_nat = jnp.concatenate([w_in_l[:, :OFF_V], w_in_l[:, OFF_GP:]], axis=1).astype(BF16)
        w_f = _cum_lane_layout(w_in_l[:, OFF_F:OFF_GP]).astype(BF16)
        w_vt = w_in_l[:, OFF_V:OFF_F].T.astype(BF16)
        b_f = _cum_lane_layout(b_forget[l]).reshape(1, CUM_LANES)
        q, k, vt, kc, gpy, gf = _mix_in(
            x, row(mix_pre_g[l]), w_nat, w_f, w_vt, b_f, _block_diag_pairs(pool_w[l]).astype(BF16),
            row(pool_scale[l]), w_pool_br[l].astype(BF16))
        o = _fox_attention(q, k, kc, vt)
        x = _mix_out(o, gpy, gf, x, kv_all, l, w_fox_br[l].astype(BF16), w_mix_out[l].astype(BF16),
                     w_xq[l].astype(BF16), w_xo[l].astype(BF16), row(mix_post_g[l]),
                     row(xa_pre_g[l]), row(xa_post_g[l]))
        x = _conv_ffn(
            x, row(ffn_pre_g[l]), row(ffn_post_g[l]),
            _chunk_cols(w_up[l][:, :D_FF]).astype(BF16), _chunk_cols(w_up[l][:, D_FF:]).astype(BF16),
            _chunk_cols(conv_w[l][:, :D_FF]), _chunk_cols(conv_w[l][:, D_FF:]),
            _chunk_cols(conv_b[l][None, :D_FF]), _chunk_cols(conv_b[l][None, D_FF:]),
            w_down[l].astype(BF16))
    return x
```

```python
import functools
import math

import jax
import jax.numpy as jnp
import numpy as np
from jax import lax
from jax.experimental import pallas as pl
from jax.experimental.pallas import tpu as pltpu

F32 = jnp.float32
BF16 = jnp.bfloat16

D_MODEL = 1024
POOL_WINDOWS = (2, 4, 8, 16)
POOL_GROUP_DIM = 128
POOL_W = len(POOL_WINDOWS) * POOL_GROUP_DIM
MAX_POOL_WINDOW = max(POOL_WINDOWS)
FOX_HEADS = 8
FOX_HEAD_DIM = 64
FOX_W = FOX_HEADS * FOX_HEAD_DIM
OFF_Q = POOL_W
OFF_K = OFF_Q + FOX_W
OFF_V = OFF_K + FOX_W
OFF_F = OFF_V + FOX_W
OFF_GP = OFF_F + FOX_HEADS
OFF_GF = OFF_GP + D_MODEL
N_MEM = 256
X_HEADS = 4
X_HEAD_DIM = 128
X_W = X_HEADS * X_HEAD_DIM
D_FF = 2816
RMS_EPS = 1e-6

V7X_LANES = 128
V7X_SUBLANES = 8
V7X_VMEM_BYTES = 64 * 1024 * 1024

N_PIECES = 3
CUM_GROUPS = N_PIECES * N_PIECES
CUM_LANES = V7X_LANES
NEG_BIG = -1e30
LOG2E = 1.4426950408889634
FOX_LOOKAHEAD = 8
FOX_KV_TILES_PER_STEP = 4
FOX_ACC_ROWS = FOX_HEAD_DIM + 16
FFN_CHUNK = 256
N_FFN_CHUNKS = D_FF // FFN_CHUNK


def _seq_tile(seq_len):
    return min(512, seq_len)


def _vmem_limit(nbytes):
    return int(min(V7X_VMEM_BYTES - 4 * 1024 * 1024, nbytes))


def _rms(xf, g):
    return xf * lax.rsqrt(jnp.mean(xf * xf, axis=-1, keepdims=True) + RMS_EPS) * g


def _dot(a, b):
    return jnp.dot(a, b, preferred_element_type=F32)


def _dot_nt(a, b):
    return lax.dot_general(a, b, (((1,), (1,)), ((), ())), preferred_element_type=F32)


def _sigmoid(z):
    return 1.0 / (1.0 + jnp.exp(-z))


def _log_sigmoid(z):
    return jnp.minimum(z, 0.0) - jnp.log(1.0 + jnp.exp(-jnp.abs(z)))


def _bf16_pieces(v):
    p1 = v.astype(BF16)
    r1 = v - p1.astype(F32)
    p2 = r1.astype(BF16)
    r2 = r1 - p2.astype(F32)
    return p1, p2, r2.astype(BF16)


def _memkv_kernel(mem_ref, g_ref, w_ref, kv_ref):
    mem_n = _rms(mem_ref[0], g_ref[0])
    kv_ref[0, 0] = _dot(mem_n.astype(BF16), w_ref[0]).astype(BF16)


def _memkv(mem, mem_g, w_xkv):
    n_layers = w_xkv.shape[0]
    batch = mem.shape[0]
    return pl.pallas_call(
        _memkv_kernel,
        grid=(n_layers, batch),
        in_specs=[
            pl.BlockSpec((1, N_MEM, D_MODEL), lambda l, b: (b, 0, 0)),
            pl.BlockSpec((1, 1, D_MODEL), lambda l, b: (l, 0, 0)),
            pl.BlockSpec((1, D_MODEL, 2 * X_W), lambda l, b: (l, 0, 0)),
        ],
        out_specs=pl.BlockSpec((1, 1, N_MEM, 2 * X_W), lambda l, b: (l, b, 0, 0)),
        out_shape=jax.ShapeDtypeStruct((n_layers, batch, N_MEM, 2 * X_W), BF16),
        compiler_params=pltpu.CompilerParams(dimension_semantics=("arbitrary", "arbitrary")),
        name="mem_kv",
    )(mem, mem_g.reshape(n_layers, 1, D_MODEL), w_xkv)


def _mixin_kernel(x_ref, g_ref, wnat_ref, wf_ref, wvt_ref, bf_ref, pwbd_ref, pscale_ref, wpbr_ref,
                  q_ref, k_ref, vt_ref, kc_ref, gpy_ref, gf_ref, ubuf_ref, ccarry_ref, *, ts):
    si = pl.program_id(1)
    carry_rows = MAX_POOL_WINDOW

    @pl.when(si == 0)
    def _():
        ubuf_ref[:, 0:carry_rows, :] = jnp.zeros((len(POOL_WINDOWS), carry_rows, POOL_GROUP_DIM), F32)
        ccarry_ref[...] = jnp.zeros_like(ccarry_ref)

    h = _rms(x_ref[0], g_ref[...]).astype(BF16)

    u = _dot(h, wnat_ref[:, 0:POOL_W])
    for gi in range(len(POOL_WINDOWS)):
        ubuf_ref[gi, carry_rows:carry_rows + ts, :] = u[:, gi * POOL_GROUP_DIM:(gi + 1) * POOL_GROUP_DIM]
    q_ref[0] = (_dot(h, wnat_ref[:, OFF_Q:OFF_K]) * (LOG2E / math.sqrt(FOX_HEAD_DIM))).astype(BF16)
    k_ref[0] = _dot(h, wnat_ref[:, OFF_K:OFF_V]).astype(BF16)
    vt_ref[0, 0] = _dot_nt(wvt_ref[...], h).astype(BF16)

    pos = si * ts + lax.broadcasted_iota(jnp.int32, (ts, 1), 0)
    mixed = []
    for pair in range(len(POOL_WINDOWS) // 2):
        pooled_pair = []
        for gi in (2 * pair, 2 * pair + 1):
            w = POOL_WINDOWS[gi]
            win = ubuf_ref[gi, pl.ds(carry_rows, ts), :]
            for j in range(1, w):
                win = win + ubuf_ref[gi, pl.ds(carry_rows - j, ts), :]
            cnt = jnp.minimum(pos + 1, w).astype(F32)
            pooled = win / cnt - ubuf_ref[gi, pl.ds(carry_rows, ts), :]
            pooled_pair.append(pooled.astype(BF16))
        mixed.append(_dot(jnp.concatenate(pooled_pair, axis=1), pwbd_ref[pair]))
    ubuf_ref[:, 0:carry_rows, :] = ubuf_ref[:, ts:ts + carry_rows, :]
    mixed = jnp.concatenate(mixed, axis=1) * pscale_ref[...]
    y_pool = _dot(mixed.astype(BF16), wpbr_ref[...])

    gate_pool = _sigmoid(_dot(h, wnat_ref[:, 3 * FOX_W:3 * FOX_W + D_MODEL]))
    gpy_ref[0] = (gate_pool * y_pool).astype(BF16)
    gate_fox = _sigmoid(_dot(h, wnat_ref[:, 3 * FOX_W + D_MODEL:3 * FOX_W + 2 * D_MODEL]))
    gf_ref[0] = gate_fox.astype(BF16)

    logf = _log_sigmoid(_dot(h, wf_ref[...]) + bf_ref[...]) * LOG2E
    grp = lax.broadcasted_iota(jnp.int32, (1, CUM_LANES), 1) // FOX_HEADS
    live = grp < CUM_GROUPS
    p1, p2, p3 = _bf16_pieces(logf)
    src = grp // N_PIECES
    xin = jnp.where(src == 0, p1, jnp.where(src == 1, p2, p3))
    xin = jnp.where(live, xin, jnp.zeros_like(xin))
    tri = (lax.broadcasted_iota(jnp.int32, (ts, ts), 0)
           >= lax.broadcasted_iota(jnp.int32, (ts, ts), 1)).astype(BF16)
    cum = _dot(tri, xin) + ccarry_ref[0:1, :]
    ccarry_ref[0:1, :] = cum[ts - 1:ts, :]
    s1, s2, s3 = _bf16_pieces(cum)
    sel = grp % N_PIECES
    kc = jnp.where(sel == 0, s1, jnp.where(sel == 1, s2, s3))
    kc_ref[0] = jnp.where(live, kc, jnp.zeros_like(kc))


def _mix_in(x, g_pre, w_nat, w_f, w_vt, b_f, pw_bd, pool_scale, w_pool_br):
    batch, seq, _ = x.shape
    ts = _seq_tile(seq)
    n_nat = w_nat.shape[1]
    const = lambda b, s: (0, 0)
    const3 = lambda b, s: (0, 0, 0)
    one = pl.Buffered(1)
    tok = lambda width: pl.BlockSpec((1, ts, width), lambda b, s: (b, s, 0))
    out_shapes = (
        jax.ShapeDtypeStruct((batch, seq, FOX_W), BF16),
        jax.ShapeDtypeStruct((batch, seq, FOX_W), BF16),
        jax.ShapeDtypeStruct((batch, seq // ts, FOX_W, ts), BF16),
        jax.ShapeDtypeStruct((batch, seq, CUM_LANES), BF16),
        jax.ShapeDtypeStruct((batch, seq, D_MODEL), BF16),
        jax.ShapeDtypeStruct((batch, seq, D_MODEL), BF16),
    )
    return pl.pallas_call(
        functools.partial(_mixin_kernel, ts=ts),
        grid=(batch, seq // ts),
        in_specs=[
            tok(D_MODEL),
            pl.BlockSpec((1, D_MODEL), const),
            pl.BlockSpec((D_MODEL, n_nat), const, pipeline_mode=one),
            pl.BlockSpec((D_MODEL, CUM_LANES), const, pipeline_mode=one),
            pl.BlockSpec((FOX_W, D_MODEL), const, pipeline_mode=one),
            pl.BlockSpec((1, CUM_LANES), const),
            pl.BlockSpec((2, 2 * POOL_GROUP_DIM, 2 * POOL_GROUP_DIM), const3, pipeline_mode=one),
            pl.BlockSpec((1, POOL_W), const),
            pl.BlockSpec((POOL_W, D_MODEL), const, pipeline_mode=one),
        ],
        out_specs=(
            tok(FOX_W), tok(FOX_W),
            pl.BlockSpec((1, 1, FOX_W, ts), lambda b, s: (b, s, 0, 0)),
            tok(CUM_LANES), tok(D_MODEL), tok(D_MODEL),
        ),
        out_shape=out_shapes,
        scratch_shapes=[
            pltpu.VMEM((len(POOL_WINDOWS), MAX_POOL_WINDOW + ts, POOL_GROUP_DIM), F32),
            pltpu.VMEM((V7X_SUBLANES, CUM_LANES), F32),
        ],
        compiler_params=pltpu.CompilerParams(
            dimension_semantics=("arbitrary", "arbitrary"),
            vmem_limit_bytes=_vmem_limit(56 * 1024 * 1024)),
        name="mix_in",
    )(x, g_pre, w_nat, w_f, w_vt, b_f, pw_bd, pool_scale, w_pool_br)


def _fox_kernel(qi_ref, kb_ref, q_ref, k_ref, kc_ref, vt_ref, o_ref, acc_ref, m_ref, *, t, kv_tiles):
    step = pl.program_id(1)
    qi = qi_ref[step]
    kb = kb_ref[step]

    @pl.when(kb == 0)
    def _():
        acc_ref[...] = jnp.zeros_like(acc_ref)
        m_ref[...] = jnp.full_like(m_ref, NEG_BIG)

    lane = lax.broadcasted_iota(jnp.int32, (1, 2 * FOX_HEAD_DIM), 1)
    cum_lane = lax.broadcasted_iota(jnp.int32, (1, CUM_LANES), 1)
    ones_rows = jnp.ones((FOX_ACC_ROWS - FOX_HEAD_DIM, t), BF16)

    def process(j, masked):
        kcb = kc_ref[0, j]

        def scores(head):
            pair, a = divmod(head, 2)
            cols = slice(pair * 2 * FOX_HEAD_DIM, (pair + 1) * 2 * FOX_HEAD_DIM)
            lhs = jnp.concatenate([k_ref[0, j, :, cols], kcb], axis=1)
            qp = q_ref[0, :, cols]
            qa = jnp.where((lane // FOX_HEAD_DIM) == a, qp, jnp.zeros_like(qp))
            pick = jnp.where((cum_lane % FOX_HEADS == head)
                             & (cum_lane // FOX_HEADS < CUM_GROUPS), -1.0, 0.0)
            pick = jnp.broadcast_to(pick, (t, CUM_LANES)).astype(BF16)
            st = _dot_nt(lhs, jnp.concatenate([qa, pick], axis=1))
            if masked:
                causal = (lax.broadcasted_iota(jnp.int32, (t, t), 0)
                          <= lax.broadcasted_iota(jnp.int32, (t, t), 1))
                st = jnp.where(causal, st, NEG_BIG)
            return st

        def softmax_pv(head, st):
            m_old = m_ref[head:head + 1, :]
            m_new = jnp.maximum(m_old, jnp.max(st, axis=0, keepdims=True))
            m_ref[head:head + 1, :] = m_new
            p = jnp.exp2(st - m_new).astype(BF16)
            alpha = jnp.exp2(m_old - m_new)
            vt_aug = jnp.concatenate(
                [vt_ref[0, j, head * FOX_HEAD_DIM:(head + 1) * FOX_HEAD_DIM, :], ones_rows], axis=0)
            acc_ref[head] = alpha * acc_ref[head] + _dot(vt_aug, p)

        pending = [scores(h) for h in range(FOX_LOOKAHEAD)]
        for head in range(FOX_HEADS):
            if head + FOX_LOOKAHEAD < FOX_HEADS:
                pending.append(scores(head + FOX_LOOKAHEAD))
            softmax_pv(head, pending.pop(0))

    def key_tile(j, carry):
        ki = kb * kv_tiles + j

        @pl.when(ki < qi)
        def _():
            process(j, False)

        @pl.when(ki == qi)
        def _():
            process(j, True)

        return carry

    lax.fori_loop(0, kv_tiles, key_tile, 0)

    @pl.when(kb == qi // kv_tiles)
    def _():
        outs = []
        for head in range(FOX_HEADS):
            inv_l = 1.0 / acc_ref[head, FOX_HEAD_DIM:FOX_HEAD_DIM + 1, :]
            outs.append(acc_ref[head, 0:FOX_HEAD_DIM, :] * inv_l)
        o_ref[0] = jnp.concatenate(outs, axis=0).T.astype(BF16)


def _fox_attention(q, k, kc, vt):
    batch, seq, _ = q.shape
    t = _seq_tile(seq)
    nq = seq // t
    kv_tiles = math.gcd(FOX_KV_TILES_PER_STEP, nq)
    qi_tab = np.concatenate([np.full(i // kv_tiles + 1, i, np.int32) for i in range(nq)])
    kb_tab = np.concatenate([np.arange(i // kv_tiles + 1, dtype=np.int32) for i in range(nq)])
    grid_spec = pltpu.PrefetchScalarGridSpec(
        num_scalar_prefetch=2,
        grid=(batch, int(qi_tab.shape[0])),
        in_specs=[
            pl.BlockSpec((1, t, FOX_W), lambda b, s, qi, kb: (b, qi[s], 0)),
            pl.BlockSpec((1, kv_tiles, t, FOX_W), lambda b, s, qi, kb: (b, kb[s], 0, 0)),
            pl.BlockSpec((1, kv_tiles, t, CUM_LANES), lambda b, s, qi, kb: (b, kb[s], 0, 0)),
            pl.BlockSpec((1, kv_tiles, FOX_W, t), lambda b, s, qi, kb: (b, kb[s], 0, 0)),
        ],
        out_specs=pl.BlockSpec((1, t, FOX_W), lambda b, s, qi, kb: (b, qi[s], 0)),
        scratch_shapes=[
            pltpu.VMEM((FOX_HEADS, FOX_ACC_ROWS, t), F32),
            pltpu.VMEM((FOX_HEADS, t), F32),
        ],
    )
    return pl.pallas_call(
        functools.partial(_fox_kernel, t=t, kv_tiles=kv_tiles),
        grid_spec=grid_spec,
        out_shape=jax.ShapeDtypeStruct((batch, seq, FOX_W), BF16),
        compiler_params=pltpu.CompilerParams(
            dimension_semantics=("arbitrary", "arbitrary"),
            vmem_limit_bytes=_vmem_limit(56 * 1024 * 1024)),
        name="fox_attention",
    )(jnp.asarray(qi_tab), jnp.asarray(kb_tab), q,
      k.reshape(batch, nq, t, FOX_W), kc.reshape(batch, nq, t, CUM_LANES), vt)


def _mixout_kernel(o_ref, gpy_ref, gf_ref, x_ref, kv_ref, wfbr_ref, wmo_ref, wxq_ref, wxo_ref,
                   g_post_ref, g_xpre_ref, g_xpost_ref, out_ref):
    y_fox = _dot(o_ref[0], wfbr_ref[...])
    merged = gpy_ref[0].astype(F32) + gf_ref[0].astype(F32) * y_fox
    x1 = x_ref[0] + _rms(_dot(merged.astype(BF16), wmo_ref[...]), g_post_ref[...])

    h = _rms(x1, g_xpre_ref[...]).astype(BF16)
    qx = _dot(h, wxq_ref[...]).astype(BF16)
    kv = kv_ref[0, 0]
    heads = []
    for head in range(X_HEADS):
        cols = slice(head * X_HEAD_DIM, (head + 1) * X_HEAD_DIM)
        vcols = slice(X_W + head * X_HEAD_DIM, X_W + (head + 1) * X_HEAD_DIM)
        s = _dot_nt(qx[:, cols], kv[:, cols]) * (1.0 / math.sqrt(X_HEAD_DIM))
        p = jnp.exp(s - jnp.max(s, axis=-1, keepdims=True))
        inv_l = 1.0 / jnp.sum(p, axis=-1, keepdims=True)
        heads.append((_dot(p.astype(BF16), kv[:, vcols]) * inv_l).astype(BF16))
    xo = _dot(jnp.concatenate(heads, axis=1), wxo_ref[...])
    out_ref[0] = x1 + _rms(xo, g_xpost_ref[...])


def _mix_out(o, gpy, gf, x, kv_all, layer, w_fox_br, w_mix_out, w_xq, w_xo, g_post, g_xpre, g_xpost):
    batch, seq, _ = x.shape
    ts = _seq_tile(seq)
    const = lambda b, s: (0, 0)
    one = pl.Buffered(1)
    tok = lambda width: pl.BlockSpec((1, ts, width), lambda b, s: (b, s, 0))
    return pl.pallas_call(
        _mixout_kernel,
        grid=(batch, seq // ts),
        in_specs=[
            tok(FOX_W), tok(D_MODEL), tok(D_MODEL), tok(D_MODEL),
            pl.BlockSpec((1, 1, N_MEM, 2 * X_W), lambda b, s: (layer, b, 0, 0)),
            pl.BlockSpec((FOX_W, D_MODEL), const, pipeline_mode=one),
            pl.BlockSpec((D_MODEL, D_MODEL), const, pipeline_mode=one),
            pl.BlockSpec((D_MODEL, X_W), const, pipeline_mode=one),
            pl.BlockSpec((X_W, D_MODEL), const, pipeline_mode=one),
            pl.BlockSpec((1, D_MODEL), const),
            pl.BlockSpec((1, D_MODEL), const),
            pl.BlockSpec((1, D_MODEL), const),
        ],
        out_specs=tok(D_MODEL),
        out_shape=jax.ShapeDtypeStruct((batch, seq, D_MODEL), F32),
        compiler_params=pltpu.CompilerParams(
            dimension_semantics=("arbitrary", "arbitrary"),
            vmem_limit_bytes=_vmem_limit(56 * 1024 * 1024)),
        name="mix_out_xattn",
    )(o, gpy, gf, x, kv_all, w_fox_br, w_mix_out, w_xq, w_xo, g_post, g_xpre, g_xpost)


def _ffn_kernel(x_ref, g_pre_ref, g_post_ref, wg_ref, wu_ref, cwg_ref, cwu_ref, cbg_ref, cbu_ref,
                wd_ref, out_ref, carry_g_ref, carry_u_ref, stage_ref, act_ref, *, ts):
    si = pl.program_id(1)
    pad = V7X_SUBLANES

    @pl.when(si == 0)
    def _():
        carry_g_ref[...] = jnp.zeros_like(carry_g_ref)
        carry_u_ref[...] = jnp.zeros_like(carry_u_ref)

    x = x_ref[0]
    h = _rms(x, g_pre_ref[...]).astype(BF16)

    def conv(z, carry_ref, cw_ref, cb_ref, c, slot):
        z1, z2 = [], []
        for sl in range(FFN_CHUNK // V7X_LANES):
            lanes = slice(sl * V7X_LANES, (sl + 1) * V7X_LANES)
            stage_ref[slot, sl, 0:pad, :] = carry_ref[c, :, lanes]
            stage_ref[slot, sl, pad:pad + ts, :] = z[:, lanes]
            z1.append(stage_ref[slot, sl, pl.ds(pad - 1, ts), :])
            z2.append(stage_ref[slot, sl, pl.ds(pad - 2, ts), :])
        carry_ref[c] = z[ts - pad:ts, :]
        z1 = jnp.concatenate(z1, axis=1)
        z2 = jnp.concatenate(z2, axis=1)
        cw = cw_ref[c]
        return cw[2:3, :] * z + cw[1:2, :] * z1 + cw[0:1, :] * z2 + cb_ref[c]

    for c in range(N_FFN_CHUNKS):
        zg = conv(_dot(h, wg_ref[c]), carry_g_ref, cwg_ref, cbg_ref, c, 0)
        zu = conv(_dot(h, wu_ref[c]), carry_u_ref, cwu_ref, cbu_ref, c, 1)
        act = jax.nn.gelu(zg, approximate=True) * zu
        act_ref[:, c * FFN_CHUNK:(c + 1) * FFN_CHUNK] = act.astype(BF16)

    y = _dot(act_ref[...], wd_ref[...])
    out_ref[0] = x + _rms(y, g_post_ref[...])


def _conv_ffn(x, g_pre, g_post, w_gate, w_up, cw_g, cw_u, cb_g, cb_u, w_down):
    batch, seq, _ = x.shape
    ts = _seq_tile(seq)
    const = lambda b, s: (0, 0)
    const3 = lambda b, s: (0, 0, 0)
    one = pl.Buffered(1)
    tok = pl.BlockSpec((1, ts, D_MODEL), lambda b, s: (b, s, 0))
    chunked = lambda rows, cols: pl.BlockSpec((N_FFN_CHUNKS, rows, cols), const3, pipeline_mode=one)
    return pl.pallas_call(
        functools.partial(_ffn_kernel, ts=ts),
        grid=(batch, seq // ts),
        in_specs=[
            tok,
            pl.BlockSpec((1, D_MODEL), const),
            pl.BlockSpec((1, D_MODEL), const),
            chunked(D_MODEL, FFN_CHUNK), chunked(D_MODEL, FFN_CHUNK),
            chunked(3, FFN_CHUNK), chunked(3, FFN_CHUNK),
            chunked(1, FFN_CHUNK), chunked(1, FFN_CHUNK),
            pl.BlockSpec((D_FF, D_MODEL), const, pipeline_mode=one),
        ],
        out_specs=tok,
        out_shape=jax.ShapeDtypeStruct((batch, seq, D_MODEL), F32),
        scratch_shapes=[
            pltpu.VMEM((N_FFN_CHUNKS, V7X_SUBLANES, FFN_CHUNK), F32),
            pltpu.VMEM((N_FFN_CHUNKS, V7X_SUBLANES, FFN_CHUNK), F32),
            pltpu.VMEM((2, FFN_CHUNK // V7X_LANES, V7X_SUBLANES + ts, V7X_LANES), F32),
            pltpu.VMEM((ts, D_FF), BF16),
        ],
        compiler_params=pltpu.CompilerParams(
            dimension_semantics=("arbitrary", "arbitrary"),
            vmem_limit_bytes=_vmem_limit(56 * 1024 * 1024)),
        name="conv_ffn",
    )(x, g_pre, g_post, w_gate, w_up, cw_g, cw_u, cb_g, cb_u, w_down)


def _chunk_cols(w):
    rows = w.shape[0]
    return w.reshape(rows, N_FFN_CHUNKS, FFN_CHUNK).transpose(1, 0, 2)


def _cum_lane_layout(w):
    rep = jnp.concatenate([w] * CUM_GROUPS, axis=-1)
    pad = [(0, 0)] * (w.ndim - 1) + [(0, CUM_LANES - CUM_GROUPS * FOX_HEADS)]
    return jnp.pad(rep, pad)


def _block_diag_pairs(pw):
    z = jnp.zeros((POOL_GROUP_DIM, POOL_GROUP_DIM), pw.dtype)
    return jnp.stack([
        jnp.concatenate([jnp.concatenate([pw[2 * i], z], axis=1),
                         jnp.concatenate([z, pw[2 * i + 1]], axis=1)], axis=0)
        for i in range(len(POOL_WINDOWS) // 2)])


def kernel(x, mem, mix_pre_g, mix_post_g, w_in, b_forget, pool_w, pool_scale, w_pool_br, w_fox_br,
           w_mix_out, xa_pre_g, xa_post_g, mem_g, w_xq, w_xkv, w_xo, ffn_pre_g, ffn_post_g, w_up,
           conv_w, conv_b, w_down):
    n_layers = w_in.shape[0]
    row = lambda v: v.reshape(1, -1)
    kv_all = _memkv(mem, mem_g, w_xkv.astype(BF16))
    for l in range(n_layers):
        w_in_l = w_in[l]
        w_nat = jnp.concatenate([w_in_l[:, :OFF_V], w_in_l[:, OFF_GP:]], axis=1).astype(BF16)
        w_f = _cum_lane_layout(w_in_l[:, OFF_F:OFF_GP]).astype(BF16)
        w_vt = w_in_l[:, OFF_V:OFF_F].T.astype(BF16)
        b_f = _cum_lane_layout(b_forget[l]).reshape(1, CUM_LANES)
        q, k, vt, kc, gpy, gf = _mix_in(
            x, row(mix_pre_g[l]), w_nat, w_f, w_vt, b_f, _block_diag_pairs(pool_w[l]).astype(BF16),
            row(pool_scale[l]), w_pool_br[l].astype(BF16))
        o = _fox_attention(q, k, kc, vt)
        x = _mix_out(o, gpy, gf, x, kv_all, l, w_fox_br[l].astype(BF16), w_mix_out[l].astype(BF16),
                     w_xq[l].astype(BF16), w_xo[l].astype(BF16), row(mix_post_g[l]),
                     row(xa_pre_g[l]), row(xa_post_g[l]))
        x = _conv_ffn(
            x, row(ffn_pre_g[l]), row(ffn_post_g[l]),
            _chunk_cols(w_up[l][:, :D_FF]).astype(BF16), _chunk_cols(w_up[l][:, D_FF:]).astype(BF16),
            _chunk_cols(conv_w[l][:, :D_FF]), _chunk_cols(conv_w[l][:, D_FF:]),
            _chunk_cols(conv_b[l][None, :D_FF]), _chunk_cols(conv_b[l][None, D_FF:]),
            w_down[l].astype(BF16))
    return x
```
